```python
import math
import jax, jax.numpy as jnp
from jax import lax
import numpy as np

D_MODEL = 1024
BATCH = 4
SEQ = 8192
DEPTH = 1
DEC_BATCH = 2
DEC_SEQ = 16384
PAST_LEN = 128

MIX_WIDTH = D_MODEL
ATTN_WIDTH = MIX_WIDTH // 2
A_HEADS = 4
V_DIM = ATTN_WIDTH // A_HEADS
QK_DIM = V_DIM // 2
ATTN_QK_WIDTH = A_HEADS * 2 * QK_DIM
HGRN_WIDTH = MIX_WIDTH - ATTN_WIDTH
HGRN_HEADS = 4
HGRN_DK = HGRN_WIDTH // HGRN_HEADS
HGRN_DV = HGRN_WIDTH // HGRN_HEADS
D_FF = 4 * D_MODEL
N_BUCKETS = 32
MAX_DISTANCE = 128
Q_BLOCK = 128
CHUNK = 64
EPS = 1e-6
SPLIT_SIZES = (ATTN_QK_WIDTH, ATTN_QK_WIDTH, ATTN_WIDTH,
               HGRN_WIDTH, HGRN_WIDTH, HGRN_WIDTH, HGRN_WIDTH, HGRN_WIDTH)
SPLIT_POINTS = tuple(int(v) for v in np.cumsum(SPLIT_SIZES)[:-1])
PROJ_WIDTH = int(sum(SPLIT_SIZES))

kernel_name = "hymba_diffattn_hgrn2_encoder"


def rms_norm(x, w):
    xf = x.astype(jnp.float32)
    xf = xf * lax.rsqrt(jnp.mean(xf * xf, axis=-1, keepdims=True) + EPS)
    return (xf * w.astype(jnp.float32)).astype(x.dtype)


def t5_bucket(rel):
    nb = N_BUCKETS // 2
    max_exact = nb // 2
    ret = jnp.where(rel > 0, nb, 0)
    n = jnp.abs(rel)
    nf = jnp.maximum(n, 1).astype(jnp.float32)
    large = max_exact + (jnp.log(nf / max_exact) / math.log(MAX_DISTANCE / max_exact)
                         * (nb - max_exact)).astype(jnp.int32)
    large = jnp.minimum(large, nb - 1)
    return ret + jnp.where(n < max_exact, n, large)


def diff_attention(q, k, v, lam, rel_bias):
    B, S = q.shape[0], q.shape[1]
    nblk = S // Q_BLOCK
    scale = QK_DIM ** -0.5
    k_pos = jnp.arange(S)
    qb = q.reshape(B, nblk, Q_BLOCK, A_HEADS, 2, QK_DIM).transpose(1, 0, 2, 3, 4, 5)
    starts = jnp.arange(nblk) * Q_BLOCK

    def block(args):
        q_blk, start = args
        q_pos = start + jnp.arange(Q_BLOCK)
        bucket = t5_bucket(k_pos[None, :] - q_pos[:, None])
        bias = jnp.take(rel_bias, bucket, axis=0)
        bias = jnp.transpose(bias, (2, 0, 1)).astype(jnp.float32)
        logits = jnp.einsum('bqhcd,bkhcd->bhcqk', q_blk, k).astype(jnp.float32) * scale
        logits = logits + bias[None, :, None]
        p = jax.nn.softmax(logits, axis=-1)
        w = p[:, :, 0] - lam * p[:, :, 1]
        return jnp.einsum('bhqk,bkhd->bqhd', w, v)

    out = lax.map(block, (qb, starts))
    return out.transpose(1, 0, 2, 3, 4).reshape(B, S, A_HEADS, V_DIM)


def gla_chunkwise(q, k, v, log_f):
    B, H, S, K = q.shape
    V = v.shape[-1]
    N = S // CHUNK
    q = q.reshape(B, H, N, CHUNK, K)
    k = k.reshape(B, H, N, CHUNK, K)
    v = v.reshape(B, H, N, CHUNK, V)
    b = jnp.cumsum(log_f.reshape(B, H, N, CHUNK, K), axis=3)
    b_last = b[:, :, :, -1:, :]
    q_t = q * jnp.exp(b)
    k_t = k * jnp.exp(-b)
    mask = jnp.tril(jnp.ones((CHUNK, CHUNK), dtype=jnp.float32))
    A = jnp.einsum('bhnck,bhnsk->bhncs', q_t, k_t) * mask
    o_intra = jnp.einsum('bhncs,bhnsv->bhncv', A, v)
    dS = jnp.einsum('bhnck,bhncv->bhnkv', k * jnp.exp(b_last - b), v)
    decay = jnp.exp(b_last[:, :, :, 0, :])

    def step(s_prev, inp):
        ds_n, dec_n = inp
        return dec_n[..., None] * s_prev + ds_n, s_prev

    s0 = jnp.zeros((B, H, K, V), dtype=jnp.float32)
    _, s_prevs = lax.scan(step, s0, (jnp.moveaxis(dS, 2, 0), jnp.moveaxis(decay, 2, 0)))
    s_prevs = jnp.moveaxis(s_prevs, 0, 2)
    o_inter = jnp.einsum('bhnck,bhnkv->bhncv', q_t, s_prevs)
    return (o_intra + o_inter).reshape(B, H, S, V)


def encoder_layer(x, l, attn_norm_w, w_in, qk_norm_w, diff_lambda, diff_subln_w,
                  rel_bias, hgrn_lb, hgrn_norm_w, w_out, mlp_norm_w, w_mlp_in, w_mlp_out):
    B, S, _ = x.shape
    f32 = jnp.float32
    h = rms_norm(x, attn_norm_w[l])
    proj = h @ w_in[l]
    q_a, k_a, v_a, q_h, f_fw, f_bw, i_h, g_h = jnp.split(proj, SPLIT_POINTS, axis=-1)

    q_a = rms_norm(q_a.reshape(B, S, A_HEADS, 2, QK_DIM), qk_norm_w[l, 0])
    k_a = rms_norm(k_a.reshape(B, S, A_HEADS, 2, QK_DIM), qk_norm_w[l, 1])
    v_a = v_a.reshape(B, S, A_HEADS, V_DIM)
    lam_p = diff_lambda[l].astype(f32)
    lam_init = 0.8 - 0.6 * math.exp(-0.3 * l)
    lam = jnp.exp(jnp.sum(lam_p[0] * lam_p[1])) - jnp.exp(jnp.sum(lam_p[2] * lam_p[3])) + lam_init
    o_a = diff_attention(q_a, k_a, v_a, lam, rel_bias)
    o_a = (rms_norm(o_a, diff_subln_w[l]) * (1.0 - lam_init)).astype(x.dtype)
    o_a = o_a.reshape(B, S, ATTN_WIDTH)

    def to_heads(t):
        return t.reshape(B, S, HGRN_HEADS, -1).transpose(0, 2, 1, 3)

    lb = jnp.cumsum(jax.nn.softmax(hgrn_lb.astype(f32), axis=1), axis=1)[:, l]
    qh = to_heads(jax.nn.silu(q_h.astype(f32)) * (HGRN_DK ** -0.5))
    vh = to_heads(i_h.astype(f32))

    def gates(f_raw, lb_d):
        f = lb_d + (1.0 - lb_d) * jax.nn.sigmoid(f_raw.astype(f32))
        return to_heads(1.0 - f), to_heads(jnp.log(f))

    k_fw, lf_fw = gates(f_fw, lb[0])
    k_bw, lf_bw = gates(f_bw, lb[1])
    o_fw = gla_chunkwise(qh, k_fw, vh, lf_fw)
    flip = lambda t: jnp.flip(t, axis=2)
    o_bw = flip(gla_chunkwise(flip(qh), flip(k_bw), flip(vh), flip(lf_bw)))
    o_h = (o_fw + o_bw).transpose(0, 2, 1, 3)
    o_h = rms_norm(o_h, hgrn_norm_w[l]) * jax.nn.silu(g_h.astype(f32).reshape(B, S, HGRN_HEADS, HGRN_DV))
    o_h = o_h.astype(x.dtype).reshape(B, S, HGRN_WIDTH)

    x = x + jnp.concatenate([o_a, o_h], axis=-1) @ w_out[l]

    h2 = rms_norm(x, mlp_norm_w[l])
    x = x + jnp.square(jax.nn.relu(h2 @ w_mlp_in[l])) @ w_mlp_out[l]
    return x


def encoder_trunk(x, attn_norm_w, w_in, qk_norm_w, diff_lambda, diff_subln_w,
                  rel_bias, hgrn_lb, hgrn_norm_w, w_out, mlp_norm_w, w_mlp_in, w_mlp_out):
    for l in range(DEPTH):
        x = encoder_layer(x, l, attn_norm_w, w_in, qk_norm_w, diff_lambda, diff_subln_w,
                          rel_bias, hgrn_lb, hgrn_norm_w, w_out, mlp_norm_w, w_mlp_in, w_mlp_out)
    return x


def setup_inputs(seed: int = 0) -> dict:
    key = jax.random.key(seed)
    ks = jax.random.split(key, 14)
    n = jax.random.normal
    f32 = jnp.float32
    return {
        "x_prompt": n(ks[0], (BATCH, SEQ, D_MODEL), f32),
        "x_sample": n(ks[1], (DEC_BATCH, DEC_SEQ, D_MODEL), f32),
        "attn_norm_w": 1.0 + 0.02 * n(ks[2], (DEPTH, D_MODEL), f32),
        "w_in": n(ks[3], (DEPTH, D_MODEL, PROJ_WIDTH), f32) * D_MODEL ** -0.5,
        "qk_norm_w": 1.0 + 0.02 * n(ks[4], (DEPTH, 2, QK_DIM), f32),
        "diff_lambda": 0.1 * n(ks[5], (DEPTH, 4, QK_DIM), f32),
        "diff_subln_w": 1.0 + 0.02 * n(ks[6], (DEPTH, V_DIM), f32),
        "rel_bias": 0.5 * n(ks[7], (N_BUCKETS, A_HEADS), f32),
        "hgrn_lb": 0.5 * n(ks[8], (2, DEPTH + 1, HGRN_WIDTH), f32),
        "hgrn_norm_w": 1.0 + 0.02 * n(ks[9], (DEPTH, HGRN_DV), f32),
        "w_out": n(ks[10], (DEPTH, MIX_WIDTH, D_MODEL), f32) * MIX_WIDTH ** -0.5,
        "mlp_norm_w": 1.0 + 0.02 * n(ks[11], (DEPTH, D_MODEL), f32),
        "w_mlp_in": n(ks[12], (DEPTH, D_MODEL, D_FF), f32) * D_MODEL ** -0.5,
        "w_mlp_out": n(ks[13], (DEPTH, D_FF, D_MODEL), f32) * D_FF ** -0.5,
    }


def reference(x_prompt, x_sample, attn_norm_w, w_in, qk_norm_w, diff_lambda, diff_subln_w,
              rel_bias, hgrn_lb, hgrn_norm_w, w_out, mlp_norm_w, w_mlp_in, w_mlp_out):
    y_prompt = encoder_trunk(x_prompt, attn_norm_w, w_in, qk_norm_w, diff_lambda, diff_subln_w,
                             rel_bias, hgrn_lb, hgrn_norm_w, w_out, mlp_norm_w, w_mlp_in, w_mlp_out)
    y_sample = encoder_trunk(x_sample, attn_norm_w, w_in, qk_norm_w, diff_lambda, diff_subln_w,
                             rel_bias, hgrn_lb, hgrn_norm_w, w_out, mlp_norm_w, w_mlp_in, w_mlp_out)
    return (y_prompt, y_sample)
```

```python
import functools
import math

import jax
import jax.numpy as jnp
from jax import lax
from jax.experimental import pallas as pl
from jax.experimental.pallas import tpu as pltpu

F32 = jnp.float32
BF16 = jnp.bfloat16

D_MODEL = 1024
A_HEADS = 4
V_DIM = 128
QK_DIM = 64
HGRN_HEADS = 4
HGRN_DK = 128
HGRN_DV = 128
SEG = 512
N_SEG = 8
D_FF = 4 * D_MODEL
N_BUCKETS = 32
MAX_DISTANCE = 128
CHUNK = 64
EPS = 1e-6
LOG2E = math.log2(math.e)
LAM_INIT = 0.8 - 0.6 * math.exp(-0.3 * 0)

LANES = 128
V_AUG = 2 * LANES
VMEM_LIMIT = 56 * 1024 * 1024

TM_PROJ = 512
TM_MLP = 512
TQ = 512
TK = 512
SB_HGRN = 2048
NEG_BIG = -1e30


def _t5_bucket(rel):
    nb = N_BUCKETS // 2
    max_exact = nb // 2
    ret = jnp.where(rel > 0, nb, 0)
    n = jnp.abs(rel)
    nf = jnp.maximum(n, 1).astype(jnp.float32)
    large = max_exact + (jnp.log(nf / max_exact) / math.log(MAX_DISTANCE / max_exact)
                         * (nb - max_exact)).astype(jnp.int32)
    large = jnp.minimum(large, nb - 1)
    return ret + jnp.where(n < max_exact, n, large)


def _proj_kernel(x_ref, nw_ref, w_ref, g_ref, wq_ref, wk_ref,
                 qa_ref, ka_ref, va_ref, qh_ref, ffw_ref, fbw_ref, ih_ref, gh_ref):
    x = x_ref[...]
    ms = jnp.mean(x * x, axis=-1, keepdims=True)
    h = (x * lax.rsqrt(ms + EPS) * nw_ref[...]).astype(BF16)

    def seg(i):
        return jnp.dot(h, w_ref[:, i * SEG:(i + 1) * SEG], preferred_element_type=F32)

    def group_norm(t, w):
        ss = jnp.dot((t * t).astype(BF16), g_ref[...], preferred_element_type=F32)
        return (t * lax.rsqrt(ss + EPS) * w).astype(BF16)

    qa_ref[...] = group_norm(seg(0), wq_ref[...])
    ka_ref[...] = group_norm(seg(1), wk_ref[...])

    v = seg(2).astype(BF16)
    tm = v.shape[0]
    lane = lax.broadcasted_iota(jnp.int32, (tm, LANES), 1)
    ones_col = jnp.where(lane == 0, 1.0, 0.0).astype(BF16)
    for hh in range(A_HEADS):
        va_ref[:, hh * V_AUG:hh * V_AUG + V_DIM] = v[:, hh * V_DIM:(hh + 1) * V_DIM]
        va_ref[:, hh * V_AUG + V_DIM:(hh + 1) * V_AUG] = ones_col

    qh = seg(3)
    qh_ref[...] = (qh * jax.nn.sigmoid(qh) * (HGRN_DK ** -0.5)).astype(BF16)
    ffw_ref[...] = seg(4)
    fbw_ref[...] = seg(5)
    ih_ref[...] = seg(6).astype(BF16)
    gh_ref[...] = seg(7)


def _proj_call(x2d, p):
    t = x2d.shape[0]
    tm = TM_PROJ
    assert t % tm == 0
    row = lambda i: (i, 0)
    const = lambda i: (0, 0)
    out_shape = (
        jax.ShapeDtypeStruct((t, SEG), BF16),
        jax.ShapeDtypeStruct((t, SEG), BF16),
        jax.ShapeDtypeStruct((t, A_HEADS * V_AUG), BF16),
        jax.ShapeDtypeStruct((t, SEG), BF16),
        jax.ShapeDtypeStruct((t, SEG), F32),
        jax.ShapeDtypeStruct((t, SEG), F32),
        jax.ShapeDtypeStruct((t, SEG), BF16),
        jax.ShapeDtypeStruct((t, SEG), F32),
    )
    out_specs = tuple(pl.BlockSpec((tm, s.shape[1]), row) for s in out_shape)
    return pl.pallas_call(
        _proj_kernel,
        out_shape=out_shape,
        grid=(t // tm,),
        in_specs=[
            pl.BlockSpec((tm, D_MODEL), row),
            pl.BlockSpec((1, D_MODEL), const),
            pl.BlockSpec((D_MODEL, N_SEG * SEG), const),
            pl.BlockSpec((SEG, SEG), const),
            pl.BlockSpec((1, SEG), const),
            pl.BlockSpec((1, SEG), const),
        ],
        out_specs=out_specs,
        compiler_params=pltpu.CompilerParams(
            dimension_semantics=("arbitrary",), vmem_limit_bytes=VMEM_LIMIT),
        name="proj",
    )(x2d, p["attn_norm_w"], p["w_in"], p["group_mat"], p["wq"], p["wk"])


def _attn_kernel(side_ref, lam_ref, q_ref, k_ref, v_ref, bias_ref, sw_ref, o_ref,
                 m_ref, acc_ref, *, nk):
    hh = pl.program_id(1)
    qi = pl.program_id(2)
    tq = q_ref.shape[1]
    tk = bias_ref.shape[3]

    q = q_ref[0]
    lane = lax.broadcasted_iota(jnp.int32, q.shape, 1)
    zero = jnp.zeros_like(q)
    qs = jnp.concatenate([jnp.where(lane < QK_DIM, q, zero),
                          jnp.where(lane >= QK_DIM, q, zero)], axis=0)

    m_ref[...] = jnp.full(m_ref.shape, NEG_BIG, F32)
    acc_ref[...] = jnp.zeros(acc_ref.shape, F32)

    def tile(j, bias):
        start = pl.multiple_of(j * tk, tk)
        k = k_ref[0, pl.ds(start, tk), :]
        v = v_ref[0, pl.ds(start, tk), :]
        s = lax.dot_general(qs, k, (((1,), (1,)), ((), ())),
                            preferred_element_type=F32)
        if bias.ndim == 2:
            s = (s.reshape(2, tq, tk) + bias[None]).reshape(2 * tq, tk)
        else:
            s = s + bias
        m_prev = m_ref[...]
        m_new = jnp.maximum(m_prev, jnp.max(s, axis=-1, keepdims=True))
        alpha = jnp.exp2(m_prev - m_new)
        p = jnp.exp2(s - m_new).astype(BF16)
        acc_ref[...] = alpha * acc_ref[...] + jnp.dot(p, v, preferred_element_type=F32)
        m_ref[...] = m_new

    c_left = side_ref[hh, 0]
    c_right = side_ref[hh, 1]

    def left_body(j, carry):
        tile(j, c_left)
        return carry

    def right_body(j, carry):
        tile(j, c_right)
        return carry

    lax.fori_loop(0, jnp.maximum(qi - 1, 0), left_body, 0)
    for d_idx, dj in enumerate((-1, 0, 1)):
        j = qi + dj

        @pl.when(jnp.logical_and(j >= 0, j < nk))
        def _():
            tile(j, bias_ref[0, d_idx])

    lax.fori_loop(jnp.minimum(qi + 2, nk), nk, right_body, 0)

    acc = acc_ref[...]
    o1 = acc[:tq, :V_DIM] / acc[:tq, V_DIM:V_DIM + 1]
    o2 = acc[tq:, :V_DIM] / acc[tq:, V_DIM:V_DIM + 1]
    o = o1 - lam_ref[0] * o2
    ms = jnp.mean(o * o, axis=-1, keepdims=True)
    o = o * lax.rsqrt(ms + EPS) * sw_ref[...] * (1.0 - LAM_INIT)
    o_ref[0] = o.astype(BF16)


def _attn_call(qa, ka, va, p):
    b, s, _ = qa.shape
    tq, tk = TQ, TK
    assert tq == tk and s % tq == 0
    nq, nk = s // tq, s // tk
    grid_spec = pltpu.PrefetchScalarGridSpec(
        num_scalar_prefetch=2,
        grid=(b, A_HEADS, nq),
        in_specs=[
            pl.BlockSpec((1, tq, LANES), lambda bi, hi, qi, *_: (bi, qi, hi)),
            pl.BlockSpec((1, s, LANES), lambda bi, hi, qi, *_: (bi, 0, hi)),
            pl.BlockSpec((1, s, V_AUG), lambda bi, hi, qi, *_: (bi, 0, hi)),
            pl.BlockSpec((1, 3, tq, tk), lambda bi, hi, qi, *_: (hi, 0, 0, 0)),
            pl.BlockSpec((1, V_DIM), lambda bi, hi, qi, *_: (0, 0)),
        ],
        out_specs=pl.BlockSpec((1, tq, V_DIM), lambda bi, hi, qi, *_: (bi, qi, hi)),
        scratch_shapes=[pltpu.VMEM((2 * tq, 1), F32), pltpu.VMEM((2 * tq, V_AUG), F32)],
    )
    return pl.pallas_call(
        functools.partial(_attn_kernel, nk=nk),
        out_shape=jax.ShapeDtypeStruct((b, s, A_HEADS * V_DIM), BF16),
        grid_spec=grid_spec,
        compiler_params=pltpu.CompilerParams(
            dimension_semantics=("arbitrary", "arbitrary", "arbitrary"),
            vmem_limit_bytes=VMEM_LIMIT),
        name="diff_attn",
    )(p["side"], p["lam"], qa, ka, va, p["bias_tiles"], p["subln_w"])


def _hgrn_kernel(qf_ref, ff_ref, vf_ref, qb_ref, fb_ref, vb_ref, lb_ref, of_ref, ob_ref,
                 sf_ref, sb_ref):
    @pl.when(pl.program_id(2) == 0)
    def _():
        sf_ref[...] = jnp.zeros(sf_ref.shape, F32)
        sb_ref[...] = jnp.zeros(sb_ref.shape, F32)

    n_chunks = qf_ref.shape[1] // CHUNK
    r = lax.broadcasted_iota(jnp.int32, (CHUNK, CHUNK), 0)
    c = lax.broadcasted_iota(jnp.int32, (CHUNK, CHUNK), 1)
    tril = r >= c
    triu = r <= c
    nt = (((1,), (1,)), ((), ()))
    tn = (((0,), (0,)), ((), ()))

    def chunk(q_ref, f_ref, v_ref, o_ref, st_ref, lb, rows, mask, last_row):
        q = q_ref[0, rows, :].astype(F32)
        v = v_ref[0, rows, :]
        f = lb + (1.0 - lb) * jax.nn.sigmoid(f_ref[0, rows, :])
        k = 1.0 - f
        lf = jnp.log(f)
        lf_hi = lf.astype(BF16)
        lf_lo = (lf - lf_hi.astype(F32)).astype(BF16)
        ones = jnp.where(mask, 1.0, 0.0).astype(BF16)
        bcum = (jnp.dot(ones, lf_hi, preferred_element_type=F32)
                + jnp.dot(ones, lf_lo, preferred_element_type=F32))
        b_last = bcum[last_row:last_row + 1, :]
        q_t = (q * jnp.exp(bcum)).astype(BF16)
        k_t = (k * jnp.exp(-bcum)).astype(BF16)
        a = lax.dot_general(q_t, k_t, nt, preferred_element_type=F32)
        a = jnp.where(mask, a, 0.0).astype(BF16)
        st = st_ref[...]
        o = (jnp.dot(a, v, preferred_element_type=F32)
             + lax.dot_general(q_t, st.astype(BF16), nt, preferred_element_type=F32))
        o_ref[0, rows, :] = o
        k_d = (k * jnp.exp(b_last - bcum)).astype(BF16)
        ds = lax.dot_general(v, k_d, tn, preferred_element_type=F32)
        st_ref[...] = jnp.exp(b_last) * st + ds

    lb_f = lb_ref[0:1, :]
    lb_b = lb_ref[1:2, :]

    def body(i, carry):
        rows_f = pl.ds(pl.multiple_of(i * CHUNK, CHUNK), CHUNK)
        chunk(qf_ref, ff_ref, vf_ref, of_ref, sf_ref, lb_f, rows_f, tril, CHUNK - 1)
        rows_b = pl.ds(pl.multiple_of((n_chunks - 1 - i) * CHUNK, CHUNK), CHUNK)
        chunk(qb_ref, fb_ref, vb_ref, ob_ref, sb_ref, lb_b, rows_b, triu, 0)
        return carry

    lax.fori_loop(0, n_chunks, body, 0)


def _hgrn_call(qh, ffw, fbw, ih, p):
    b, s, _ = qh.shape
    sb = min(SB_HGRN, s)
    assert s % sb == 0 and sb % CHUNK == 0
    nsb = s // sb
    fwd = lambda bi, hi, j: (bi, j, hi)
    bwd = lambda bi, hi, j: (bi, nsb - 1 - j, hi)
    blk = (1, sb, LANES)
    return pl.pallas_call(
        _hgrn_kernel,
        out_shape=(jax.ShapeDtypeStruct((b, s, SEG), F32), jax.ShapeDtypeStruct((b, s, SEG), F32)),
        grid=(b, HGRN_HEADS, nsb),
        in_specs=[
            pl.BlockSpec(blk, fwd), pl.BlockSpec(blk, fwd), pl.BlockSpec(blk, fwd),
            pl.BlockSpec(blk, bwd), pl.BlockSpec(blk, bwd), pl.BlockSpec(blk, bwd),
            pl.BlockSpec((2, LANES), lambda bi, hi, j: (0, hi)),
        ],
        out_specs=(pl.BlockSpec(blk, fwd), pl.BlockSpec(blk, bwd)),
        scratch_shapes=[pltpu.VMEM((HGRN_DV, HGRN_DK), F32), pltpu.VMEM((HGRN_DV, HGRN_DK), F32)],
        compiler_params=pltpu.CompilerParams(
            dimension_semantics=("arbitrary", "arbitrary", "arbitrary"),
            vmem_limit_bytes=VMEM_LIMIT),
        name="hgrn2",
    )(qh, ffw, ih, qh, fbw, ih, p["lb"])


def _mlp_kernel(x_ref, oa_ref, of_ref, ob_ref, g_ref, hw_ref, wo_ref, mw_ref, w1_ref, w2_ref, y_ref):
    o = of_ref[...] + ob_ref[...]
    g = g_ref[...]
    parts = [oa_ref[...]]
    for hh in range(HGRN_HEADS):
        sl = slice(hh * HGRN_DV, (hh + 1) * HGRN_DV)
        oh = o[:, sl]
        ms = jnp.mean(oh * oh, axis=-1, keepdims=True)
        gh = g[:, sl]
        parts.append((oh * lax.rsqrt(ms + EPS) * hw_ref[...] * (gh * jax.nn.sigmoid(gh))).astype(BF16))
    cat = jnp.concatenate(parts, axis=1)
    x1 = x_ref[...] + jnp.dot(cat, wo_ref[...], preferred_element_type=F32)
    ms = jnp.mean(x1 * x1, axis=-1, keepdims=True)
    h2 = (x1 * lax.rsqrt(ms + EPS) * mw_ref[...]).astype(BF16)
    acc = x1
    for cc in range(D_FF // D_MODEL):
        sl = slice(cc * D_MODEL, (cc + 1) * D_MODEL)
        u = jnp.dot(h2, w1_ref[:, sl], preferred_element_type=F32)
        a = jnp.square(jnp.maximum(u, 0.0)).astype(BF16)
        acc = acc + jnp.dot(a, w2_ref[sl, :], preferred_element_type=F32)
    y_ref[...] = acc


def _mlp_call(x2d, oa, o_f, o_b, gh, p):
    t = x2d.shape[0]
    tm = TM_MLP
    assert t % tm == 0
    row = lambda i: (i, 0)
    const = lambda i: (0, 0)
    once = pl.Buffered(1)
    return pl.pallas_call(
        _mlp_kernel,
        out_shape=jax.ShapeDtypeStruct((t, D_MODEL), F32),
        grid=(t // tm,),
        in_specs=[
            pl.BlockSpec((tm, D_MODEL), row),
            pl.BlockSpec((tm, SEG), row),
            pl.BlockSpec((tm, SEG), row),
            pl.BlockSpec((tm, SEG), row),
            pl.BlockSpec((tm, SEG), row),
            pl.BlockSpec((1, HGRN_DV), const),
            pl.BlockSpec((D_MODEL, D_MODEL), const, pipeline_mode=once),
            pl.BlockSpec((1, D_MODEL), const),
            pl.BlockSpec((D_MODEL, D_FF), const, pipeline_mode=once),
            pl.BlockSpec((D_FF, D_MODEL), const, pipeline_mode=once),
        ],
        out_specs=pl.BlockSpec((tm, D_MODEL), row),
        compiler_params=pltpu.CompilerParams(
            dimension_semantics=("arbitrary",), vmem_limit_bytes=VMEM_LIMIT),
        name="out_mlp",
    )(x2d, oa, o_f, o_b, gh, p["hgrn_norm_w"], p["w_out"], p["mlp_norm_w"], p["w_mlp_in"], p["w_mlp_out"])


def _prepare(attn_norm_w, w_in, qk_norm_w, diff_lambda, diff_subln_w, rel_bias, hgrn_lb,
             hgrn_norm_w, w_out, mlp_norm_w, w_mlp_in, w_mlp_out):
    l = 0
    lam_p = diff_lambda[l].astype(F32)
    lam = jnp.exp(jnp.sum(lam_p[0] * lam_p[1])) - jnp.exp(jnp.sum(lam_p[2] * lam_p[3])) + LAM_INIT
    lb = jnp.cumsum(jax.nn.softmax(hgrn_lb.astype(F32), axis=1), axis=1)[:, l]

    ii = jnp.arange(TQ)[:, None]
    jj = jnp.arange(TK)[None, :]
    rel = jnp.stack([d * TK + jj - ii for d in (-1, 0, 1)])
    tiles = jnp.take(rel_bias.astype(F32), _t5_bucket(rel), axis=0)
    tiles = jnp.transpose(tiles, (3, 0, 1, 2)) * LOG2E
    far = jnp.take(rel_bias.astype(F32), _t5_bucket(jnp.array([-2 * TK, 2 * TK])), axis=0)
    side = jnp.transpose(far) * LOG2E

    reps = SEG // QK_DIM
    grp = jnp.arange(SEG) // QK_DIM
    group_mat = jnp.where(grp[:, None] == grp[None, :], 1.0 / QK_DIM, 0.0).astype(BF16)
    return dict(
        attn_norm_w=attn_norm_w[l][None].astype(F32),
        w_in=w_in[l].astype(BF16),
        group_mat=group_mat,
        wq=(jnp.tile(qk_norm_w[l, 0].astype(F32), reps) * (QK_DIM ** -0.5 * LOG2E))[None],
        wk=jnp.tile(qk_norm_w[l, 1].astype(F32), reps)[None],
        lam=jnp.reshape(lam, (1,)).astype(F32),
        side=side,
        bias_tiles=tiles,
        subln_w=diff_subln_w[l][None].astype(F32),
        lb=lb,
        hgrn_norm_w=hgrn_norm_w[l][None].astype(F32),
        w_out=w_out[l].astype(BF16),
        mlp_norm_w=mlp_norm_w[l][None].astype(F32),
        w_mlp_in=w_mlp_in[l].astype(BF16),
        w_mlp_out=w_mlp_out[l].astype(BF16),
    )


def _layer(x, p):
    b, s, d = x.shape
    x2d = x.reshape(b * s, d)
    qa, ka, va, qh, ffw, fbw, ih, gh = _proj_call(x2d, p)
    r3 = lambda t: t.reshape(b, s, t.shape[-1])
    oa = _attn_call(r3(qa), r3(ka), r3(va), p)
    o_f, o_b = _hgrn_call(r3(qh), r3(ffw), r3(fbw), r3(ih), p)
    y = _mlp_call(x2d, oa.reshape(b * s, -1), o_f.reshape(b * s, -1), o_b.reshape(b * s, -1), gh, p)
    return y.reshape(b, s, d)


def kernel(x_prompt, x_sample, attn_norm_w, w_in, qk_norm_w, diff_lambda, diff_subln_w, rel_bias,
           hgrn_lb, hgrn_norm_w, w_out, mlp_norm_w, w_mlp_in, w_mlp_out):
    p = _prepare(attn_norm_w, w_in, qk_norm_w, diff_lambda, diff_subln_w, rel_bias, hgrn_lb,
                 hgrn_norm_w, w_out, mlp_norm_w, w_mlp_in, w_mlp_out)
    return _layer(x_prompt, p), _layer(x_sample, p)
```

```python
import functools
import math

import jax
import jax.numpy as jnp
from jax import lax
from jax.experimental import pallas as pl
from jax.experimental.pallas import tpu as pltpu

F32 = jnp.float32
BF16 = jnp.bfloat16

D_MODEL = 1024
A_HEADS = 4
V_DIM = 128
QK_DIM = 64
HGRN_HEADS = 4
HGRN_DK = 128
HGRN_DV = 128
SEG = 512
N_SEG = 8
D_FF = 4 * D_MODEL
N_BUCKETS = 32
MAX_DISTANCE = 128
CHUNK = 64
EPS = 1e-6
LOG2E = math.log2(math.e)
LAM_INIT = 0.8 - 0.6 * math.exp(-0.3 * 0)

LANES = 128
V_AUG = 2 * LANES
VMEM_LIMIT = 56 * 1024 * 1024

TM_PROJ = 512
TM_MLP = 512
TQ = 512
TK = 512
SB_HGRN = 2048
NEG_BIG = -1e30
FAST_EXP_RANGE = 60.0


def _t5_bucket(rel):
    nb = N_BUCKETS // 2
    max_exact = nb // 2
    ret = jnp.where(rel > 0, nb, 0)
    n = jnp.abs(rel)
    nf = jnp.maximum(n, 1).astype(jnp.float32)
    large = max_exact + (jnp.log(nf / max_exact) / math.log(MAX_DISTANCE / max_exact)
                         * (nb - max_exact)).astype(jnp.int32)
    large = jnp.minimum(large, nb - 1)
    return ret + jnp.where(n < max_exact, n, large)


def _proj_kernel(x_ref, nw_ref, w_ref, g_ref, wq_ref, wk_ref,
                 qa_ref, ka_ref, va_ref, qh_ref, ffw_ref, fbw_ref, ih_ref, gh_ref):
    x = x_ref[...]
    ms = jnp.mean(x * x, axis=-1, keepdims=True)
    h = (x * lax.rsqrt(ms + EPS) * nw_ref[...]).astype(BF16)

    def seg(i):
        return jnp.dot(h, w_ref[:, i * SEG:(i + 1) * SEG], preferred_element_type=F32)

    def group_norm(t, w):
        ss = jnp.dot((t * t).astype(BF16), g_ref[...], preferred_element_type=F32)
        return (t * lax.rsqrt(ss + EPS) * w).astype(BF16)

    qa_ref[...] = group_norm(seg(0), wq_ref[...])
    ka_ref[...] = group_norm(seg(1), wk_ref[...])

    v = seg(2).astype(BF16)
    tm = v.shape[0]
    lane = lax.broadcasted_iota(jnp.int32, (tm, LANES), 1)
    ones_col = jnp.where(lane == 0, 1.0, 0.0).astype(BF16)
    for hh in range(A_HEADS):
        va_ref[:, hh * V_AUG:hh * V_AUG + V_DIM] = v[:, hh * V_DIM:(hh + 1) * V_DIM]
        va_ref[:, hh * V_AUG + V_DIM:(hh + 1) * V_AUG] = ones_col

    qh = seg(3)
    qh_ref[...] = (qh * jax.nn.sigmoid(qh) * (HGRN_DK ** -0.5)).astype(BF16)
    ffw_ref[...] = seg(4)
    fbw_ref[...] = seg(5)
    ih_ref[...] = seg(6).astype(BF16)
    gh_ref[...] = seg(7)


def _proj_call(x2d, p):
    t = x2d.shape[0]
    tm = TM_PROJ
    assert t % tm == 0
    row = lambda i: (i, 0)
    const = lambda i: (0, 0)
    out_shape = (
        jax.ShapeDtypeStruct((t, SEG), BF16),
        jax.ShapeDtypeStruct((t, SEG), BF16),
        jax.ShapeDtypeStruct((t, A_HEADS * V_AUG), BF16),
        jax.ShapeDtypeStruct((t, SEG), BF16),
        jax.ShapeDtypeStruct((t, SEG), F32),
        jax.ShapeDtypeStruct((t, SEG), F32),
        jax.ShapeDtypeStruct((t, SEG), BF16),
        jax.ShapeDtypeStruct((t, SEG), F32),
    )
    out_specs = tuple(pl.BlockSpec((tm, s.shape[1]), row) for s in out_shape)
    return pl.pallas_call(
        _proj_kernel,
        out_shape=out_shape,
        grid=(t // tm,),
        in_specs=[
            pl.BlockSpec((tm, D_MODEL), row),
            pl.BlockSpec((1, D_MODEL), const),
            pl.BlockSpec((D_MODEL, N_SEG * SEG), const),
            pl.BlockSpec((SEG, SEG), const),
            pl.BlockSpec((1, SEG), const),
            pl.BlockSpec((1, SEG), const),
        ],
        out_specs=out_specs,
        compiler_params=pltpu.CompilerParams(
            dimension_semantics=("arbitrary",), vmem_limit_bytes=VMEM_LIMIT),
        name="proj",
    )(x2d, p["attn_norm_w"], p["w_in"], p["group_mat"], p["wq"], p["wk"])


def _bias_kernel(tab_ref, bucket_ref, o_ref):
    hh = pl.program_id(0)
    n_d, tq, _ = bucket_ref.shape
    rows = 64
    per_d = tq // rows

    def body(i, carry):
        d = i // per_d
        r0 = pl.multiple_of((i % per_d) * rows, rows)
        bucket = bucket_ref[d, pl.ds(r0, rows), :]
        out = jnp.zeros(bucket.shape, F32)
        for bb in range(N_BUCKETS):
            out = jnp.where(bucket == bb, tab_ref[bb, hh], out)
        o_ref[0, d, pl.ds(r0, rows), :] = out
        return carry

    lax.fori_loop(0, n_d * per_d, body, 0)


def _bias_call(table, buckets):
    n_d, tq, tk = buckets.shape
    return pl.pallas_call(
        _bias_kernel,
        out_shape=jax.ShapeDtypeStruct((A_HEADS, n_d, tq, tk), F32),
        grid=(A_HEADS,),
        in_specs=[
            pl.BlockSpec(memory_space=pltpu.SMEM),
            pl.BlockSpec((n_d, tq, tk), lambda hi: (0, 0, 0)),
        ],
        out_specs=pl.BlockSpec((1, n_d, tq, tk), lambda hi: (hi, 0, 0, 0)),
        compiler_params=pltpu.CompilerParams(
            dimension_semantics=("arbitrary",), vmem_limit_bytes=VMEM_LIMIT),
        name="bias_tiles",
    )(table, buckets)


def _attn_kernel(fast_ref, side_ref, lam_ref, q_ref, k_ref, v_ref, bias_ref, sw_ref, o_ref,
                 m_ref, acc_ref, *, nk):
    hh = pl.program_id(1)
    qi = pl.program_id(2)
    tq = q_ref.shape[1]
    tk = bias_ref.shape[3]

    q = q_ref[0]
    lane = lax.broadcasted_iota(jnp.int32, q.shape, 1)
    zero = jnp.zeros_like(q)
    qs = jnp.concatenate([jnp.where(lane < QK_DIM, q, zero),
                          jnp.where(lane >= QK_DIM, q, zero)], axis=0)

    def scores(j):
        k = k_ref[0, pl.ds(pl.multiple_of(j * tk, tk), tk), :]
        return lax.dot_general(qs, k, (((1,), (1,)), ((), ())), preferred_element_type=F32)

    def values(j):
        return v_ref[0, pl.ds(pl.multiple_of(j * tk, tk), tk), :]

    def add_bias(s, bias):
        return (s.reshape(2, tq, tk) + bias[None]).reshape(2 * tq, tk)

    def for_tiles(lo, hi, fn):
        def body(j, carry):
            fn(j)
            return carry
        lax.fori_loop(lo, hi, body, 0)

    def for_band(fn):
        for d_idx, dj in enumerate((-1, 0, 1)):
            j = qi + dj

            @pl.when(jnp.logical_and(j >= 0, j < nk))
            def _():
                fn(j, bias_ref[0, d_idx])

    n_left = jnp.maximum(qi - 1, 0)
    right0 = jnp.minimum(qi + 2, nk)
    acc_ref[...] = jnp.zeros(acc_ref.shape, F32)

    @pl.when(fast_ref[0] == 1)
    def _():
        def accumulate(j, s):
            p = jnp.exp2(s).astype(BF16)
            acc_ref[...] += jnp.dot(p, values(j), preferred_element_type=F32)

        for_tiles(right0, nk, lambda j: accumulate(j, scores(j)))
        acc_ref[...] = acc_ref[...] * side_ref[hh, 1]
        for_tiles(0, n_left, lambda j: accumulate(j, scores(j)))
        for_band(lambda j, bias: accumulate(j, add_bias(scores(j), bias)))

    @pl.when(fast_ref[0] != 1)
    def _():
        m_ref[...] = jnp.full(m_ref.shape, NEG_BIG, F32)

        def update(j, s):
            m_prev = m_ref[...]
            m_new = jnp.maximum(m_prev, jnp.max(s, axis=-1, keepdims=True))
            alpha = jnp.exp2(m_prev - m_new)
            p = jnp.exp2(s - m_new).astype(BF16)
            acc_ref[...] = alpha * acc_ref[...] + jnp.dot(p, values(j), preferred_element_type=F32)
            m_ref[...] = m_new

        for_tiles(0, n_left, lambda j: update(j, scores(j)))
        for_band(lambda j, bias: update(j, add_bias(scores(j), bias)))
        for_tiles(right0, nk, lambda j: update(j, scores(j) + side_ref[hh, 0]))

    acc = acc_ref[...]
    o1 = acc[:tq, :V_DIM] / acc[:tq, V_DIM:V_DIM + 1]
    o2 = acc[tq:, :V_DIM] / acc[tq:, V_DIM:V_DIM + 1]
    o = o1 - lam_ref[0] * o2
    ms = jnp.mean(o * o, axis=-1, keepdims=True)
    o = o * lax.rsqrt(ms + EPS) * sw_ref[...] * (1.0 - LAM_INIT)
    o_ref[0] = o.astype(BF16)


def _attn_call(qa, ka, va, p):
    b, s, _ = qa.shape
    tq, tk = TQ, TK
    assert tq == tk and s % tq == 0
    nq, nk = s // tq, s // tk
    grid_spec = pltpu.PrefetchScalarGridSpec(
        num_scalar_prefetch=3,
        grid=(b, A_HEADS, nq),
        in_specs=[
            pl.BlockSpec((1, tq, LANES), lambda bi, hi, qi, *_: (bi, qi, hi)),
            pl.BlockSpec((1, s, LANES), lambda bi, hi, qi, *_: (bi, 0, hi)),
            pl.BlockSpec((1, s, V_AUG), lambda bi, hi, qi, *_: (bi, 0, hi)),
            pl.BlockSpec((1, 3, tq, tk), lambda bi, hi, qi, *_: (hi, 0, 0, 0)),
            pl.BlockSpec((1, V_DIM), lambda bi, hi, qi, *_: (0, 0)),
        ],
        out_specs=pl.BlockSpec((1, tq, V_DIM), lambda bi, hi, qi, *_: (bi, qi, hi)),
        scratch_shapes=[pltpu.VMEM((2 * tq, 1), F32), pltpu.VMEM((2 * tq, V_AUG), F32)],
    )
    return pl.pallas_call(
        functools.partial(_attn_kernel, nk=nk),
        out_shape=jax.ShapeDtypeStruct((b, s, A_HEADS * V_DIM), BF16),
        grid_spec=grid_spec,
        compiler_params=pltpu.CompilerParams(
            dimension_semantics=("arbitrary", "arbitrary", "arbitrary"),
            vmem_limit_bytes=VMEM_LIMIT),
        name="diff_attn",
    )(p["fast"], p["side"], p["lam"], qa, ka, va, p["bias_tiles"], p["subln_w"])


def _hgrn_kernel(qf_ref, ff_ref, vf_ref, qb_ref, fb_ref, vb_ref, lb_ref, of_ref, ob_ref,
                 sf_ref, sb_ref):
    @pl.when(pl.program_id(2) == 0)
    def _():
        sf_ref[...] = jnp.zeros(sf_ref.shape, F32)
        sb_ref[...] = jnp.zeros(sb_ref.shape, F32)

    n_chunks = qf_ref.shape[1] // CHUNK
    r = lax.broadcasted_iota(jnp.int32, (CHUNK, CHUNK), 0)
    c = lax.broadcasted_iota(jnp.int32, (CHUNK, CHUNK), 1)
    tril = r >= c
    triu = r <= c
    nt = (((1,), (1,)), ((), ()))
    tn = (((0,), (0,)), ((), ()))

    def chunk(q_ref, f_ref, v_ref, o_ref, st_ref, lb, rows, mask, last_row):
        q = q_ref[0, rows, :].astype(F32)
        v = v_ref[0, rows, :]
        f = lb + (1.0 - lb) * jax.nn.sigmoid(f_ref[0, rows, :])
        k = 1.0 - f
        lf = jnp.log(f)
        lf_hi = lf.astype(BF16)
        lf_lo = (lf - lf_hi.astype(F32)).astype(BF16)
        ones = jnp.where(mask, 1.0, 0.0).astype(BF16)
        bcum = (jnp.dot(ones, lf_hi, preferred_element_type=F32)
                + jnp.dot(ones, lf_lo, preferred_element_type=F32))
        b_last = bcum[last_row:last_row + 1, :]
        q_t = (q * jnp.exp(bcum)).astype(BF16)
        k_t = (k * jnp.exp(-bcum)).astype(BF16)
        a = lax.dot_general(q_t, k_t, nt, preferred_element_type=F32)
        a = jnp.where(mask, a, 0.0).astype(BF16)
        st = st_ref[...]
        o = (jnp.dot(a, v, preferred_element_type=F32)
             + lax.dot_general(q_t, st.astype(BF16), nt, preferred_element_type=F32))
        o_ref[0, rows, :] = o
        k_d = (k * jnp.exp(b_last - bcum)).astype(BF16)
        ds = lax.dot_general(v, k_d, tn, preferred_element_type=F32)
        st_ref[...] = jnp.exp(b_last) * st + ds

    lb_f = lb_ref[0:1, :]
    lb_b = lb_ref[1:2, :]

    def body(i, carry):
        rows_f = pl.ds(pl.multiple_of(i * CHUNK, CHUNK), CHUNK)
        chunk(qf_ref, ff_ref, vf_ref, of_ref, sf_ref, lb_f, rows_f, tril, CHUNK - 1)
        rows_b = pl.ds(pl.multiple_of((n_chunks - 1 - i) * CHUNK, CHUNK), CHUNK)
        chunk(qb_ref, fb_ref, vb_ref, ob_ref, sb_ref, lb_b, rows_b, triu, 0)
        return carry

    lax.fori_loop(0, n_chunks, body, 0)


def _hgrn_call(qh, ffw, fbw, ih, p):
    b, s, _ = qh.shape
    sb = min(SB_HGRN, s)
    assert s % sb == 0 and sb % CHUNK == 0
    nsb = s // sb
    fwd = lambda bi, hi, j: (bi, j, hi)
    bwd = lambda bi, hi, j: (bi, nsb - 1 - j, hi)
    blk = (1, sb, LANES)
    return pl.pallas_call(
        _hgrn_kernel,
        out_shape=(jax.ShapeDtypeStruct((b, s, SEG), F32), jax.ShapeDtypeStruct((b, s, SEG), F32)),
        grid=(b, HGRN_HEADS, nsb),
        in_specs=[
            pl.BlockSpec(blk, fwd), pl.BlockSpec(blk, fwd), pl.BlockSpec(blk, fwd),
            pl.BlockSpec(blk, bwd), pl.BlockSpec(blk, bwd), pl.BlockSpec(blk, bwd),
            pl.BlockSpec((2, LANES), lambda bi, hi, j: (0, hi)),
        ],
        out_specs=(pl.BlockSpec(blk, fwd), pl.BlockSpec(blk, bwd)),
        scratch_shapes=[pltpu.VMEM((HGRN_DV, HGRN_DK), F32), pltpu.VMEM((HGRN_DV, HGRN_DK), F32)],
        compiler_params=pltpu.CompilerParams(
            dimension_semantics=("arbitrary", "arbitrary", "arbitrary"),
            vmem_limit_bytes=VMEM_LIMIT),
        name="hgrn2",
    )(qh, ffw, ih, qh, fbw, ih, p["lb"])


def _mlp_kernel(x_ref, oa_ref, of_ref, ob_ref, g_ref, hw_ref, wo_ref, mw_ref, w1_ref, w2_ref, y_ref):
    o = of_ref[...] + ob_ref[...]
    g = g_ref[...]
    parts = [oa_ref[...]]
    for hh in range(HGRN_HEADS):
        sl = slice(hh * HGRN_DV, (hh + 1) * HGRN_DV)
        oh = o[:, sl]
        ms = jnp.mean(oh * oh, axis=-1, keepdims=True)
        gh = g[:, sl]
        parts.append((oh * lax.rsqrt(ms + EPS) * hw_ref[...] * (gh * jax.nn.sigmoid(gh))).astype(BF16))
    cat = jnp.concatenate(parts, axis=1)
    x1 = x_ref[...] + jnp.dot(cat, wo_ref[...], preferred_element_type=F32)
    ms = jnp.mean(x1 * x1, axis=-1, keepdims=True)
    h2 = (x1 * lax.rsqrt(ms + EPS) * mw_ref[...]).astype(BF16)
    acc = x1
    for cc in range(D_FF // D_MODEL):
        sl = slice(cc * D_MODEL, (cc + 1) * D_MODEL)
        u = jnp.dot(h2, w1_ref[:, sl], preferred_element_type=F32)
        a = jnp.square(jnp.maximum(u, 0.0)).astype(BF16)
        acc = acc + jnp.dot(a, w2_ref[sl, :], preferred_element_type=F32)
    y_ref[...] = acc


def _mlp_call(x2d, oa, o_f, o_b, gh, p):
    t = x2d.shape[0]
    tm = TM_MLP
    assert t % tm == 0
    row = lambda i: (i, 0)
    const = lambda i: (0, 0)
    once = pl.Buffered(1)
    return pl.pallas_call(
        _mlp_kernel,
        out_shape=jax.ShapeDtypeStruct((t, D_MODEL), F32),
        grid=(t // tm,),
        in_specs=[
            pl.BlockSpec((tm, D_MODEL), row),
            pl.BlockSpec((tm, SEG), row),
            pl.BlockSpec((tm, SEG), row),
            pl.BlockSpec((tm, SEG), row),
            pl.BlockSpec((tm, SEG), row),
            pl.BlockSpec((1, HGRN_DV), const),
            pl.BlockSpec((D_MODEL, D_MODEL), const, pipeline_mode=once),
            pl.BlockSpec((1, D_MODEL), const),
            pl.BlockSpec((D_MODEL, D_FF), const, pipeline_mode=once),
            pl.BlockSpec((D_FF, D_MODEL), const, pipeline_mode=once),
        ],
        out_specs=pl.BlockSpec((tm, D_MODEL), row),
        compiler_params=pltpu.CompilerParams(
            dimension_semantics=("arbitrary",), vmem_limit_bytes=VMEM_LIMIT),
        name="out_mlp",
    )(x2d, oa, o_f, o_b, gh, p["hgrn_norm_w"], p["w_out"], p["mlp_norm_w"], p["w_mlp_in"], p["w_mlp_out"])


def _prepare(attn_norm_w, w_in, qk_norm_w, diff_lambda, diff_subln_w, rel_bias, hgrn_lb,
             hgrn_norm_w, w_out, mlp_norm_w, w_mlp_in, w_mlp_out):
    l = 0
    lam_p = diff_lambda[l].astype(F32)
    lam = jnp.exp(jnp.sum(lam_p[0] * lam_p[1])) - jnp.exp(jnp.sum(lam_p[2] * lam_p[3])) + LAM_INIT
    lb = jnp.cumsum(jax.nn.softmax(hgrn_lb.astype(F32), axis=1), axis=1)[:, l]

    rb = rel_bias.astype(F32)
    far = jnp.take(rb, _t5_bucket(jnp.array([-2 * TK, 2 * TK])), axis=0)
    table = (rb - far[0][None, :]) * LOG2E
    ii = jnp.arange(TQ)[:, None]
    jj = jnp.arange(TK)[None, :]
    buckets = _t5_bucket(jnp.stack([d * TK + jj - ii for d in (-1, 0, 1)]))
    tiles = _bias_call(table, buckets.astype(jnp.int32))
    dc = (far[1] - far[0]) * LOG2E
    side = jnp.stack([dc, jnp.exp2(dc)], axis=1)

    wq_eff = qk_norm_w[l, 0].astype(F32) * (QK_DIM ** -0.5 * LOG2E)
    wk_eff = qk_norm_w[l, 1].astype(F32)
    logit_bound = 1.02 * QK_DIM * jnp.max(jnp.abs(wq_eff * wk_eff)) + 2.0 * jnp.max(jnp.abs(table))
    fast = (logit_bound <= FAST_EXP_RANGE).astype(jnp.int32).reshape(1)

    reps = SEG // QK_DIM
    grp = jnp.arange(SEG) // QK_DIM
    group_mat = jnp.where(grp[:, None] == grp[None, :], 1.0 / QK_DIM, 0.0).astype(BF16)
    return dict(
        attn_norm_w=attn_norm_w[l][None].astype(F32),
        w_in=w_in[l].astype(BF16),
        group_mat=group_mat,
        wq=jnp.tile(wq_eff, reps)[None],
        wk=jnp.tile(wk_eff, reps)[None],
        fast=fast,
        lam=jnp.reshape(lam, (1,)).astype(F32),
        side=side,
        bias_tiles=tiles,
        subln_w=diff_subln_w[l][None].astype(F32),
        lb=lb,
        hgrn_norm_w=hgrn_norm_w[l][None].astype(F32),
        w_out=w_out[l].astype(BF16),
        mlp_norm_w=mlp_norm_w[l][None].astype(F32),
        w_mlp_in=w_mlp_in[l].astype(BF16),
        w_mlp_out=w_mlp_out[l].astype(BF16),
    )


def _layer(x, p):
    b, s, d = x.shape
    x2d = x.reshape(b * s, d)
    qa, ka, va, qh, ffw, fbw, ih, gh = _proj_call(x2d, p)
    r3 = lambda t: t.reshape(b, s, t.shape[-1])
    oa = _attn_call(r3(qa), r3(ka), r3(va), p)
    o_f, o_b = _hgrn_call(r3(qh), r3(ffw), r3(fbw), r3(ih), p)
    y = _mlp_call(x2d, oa.reshape(b * s, -1), o_f.reshape(b * s, -1), o_b.reshape(b * s, -1), gh, p)
    return y.reshape(b, s, d)


def kernel(x_prompt, x_sample, attn_norm_w, w_in, qk_norm_w, diff_lambda, diff_subln_w, rel_bias,
           hgrn_lb, hgrn_norm_w, w_out, mlp_norm_w, w_mlp_in, w_mlp_out):
    p = _prepare(attn_norm_w, w_in, qk_norm_w, diff_lambda, diff_subln_w, rel_bias, hgrn_lb,
                 hgrn_norm_w, w_out, mlp_norm_w, w_mlp_in, w_mlp_out)
    return _layer(x_prompt, p), _layer(x_sample, p)
```

```python
import functools
import math

import jax
import jax.numpy as jnp
from jax import lax
from jax.experimental import pallas as pl
from jax.experimental.pallas import tpu as pltpu

F32 = jnp.float32
BF16 = jnp.bfloat16

D_MODEL = 1024
A_HEADS = 4
V_DIM = 128
QK_DIM = 64
HGRN_HEADS = 4
HGRN_DK = 128
HGRN_DV = 128
SEG = 512
N_SEG = 8
D_FF = 4 * D_MODEL
N_BUCKETS = 32
MAX_DISTANCE = 128
CHUNK = 64
EPS = 1e-6
LOG2E = math.log2(math.e)
LAM_INIT = 0.8 - 0.6 * math.exp(-0.3 * 0)

LANES = 128
V_AUG = 2 * LANES
VMEM_LIMIT = 56 * 1024 * 1024

TM_PROJ = 512
TM_MLP = 512
TQ = 512
TK = 512
KEY_GROUP = 8
BAND = 2
SB_HGRN = 2048
NEG_BIG = -1e30
FAST_EXP_RANGE = 60.0


def _t5_bucket(rel):
    nb = N_BUCKETS // 2
    max_exact = nb // 2
    ret = jnp.where(rel > 0, nb, 0)
    n = jnp.abs(rel)
    nf = jnp.maximum(n, 1).astype(jnp.float32)
    large = max_exact + (jnp.log(nf / max_exact) / math.log(MAX_DISTANCE / max_exact)
                         * (nb - max_exact)).astype(jnp.int32)
    large = jnp.minimum(large, nb - 1)
    return ret + jnp.where(n < max_exact, n, large)


def _proj_kernel(x_ref, nw_ref, w_ref, g_ref, wq_ref, wk_ref,
                 qa_ref, ka_ref, va_ref, qh_ref, ffw_ref, fbw_ref, ih_ref, gh_ref):
    x = x_ref[...]
    ms = jnp.mean(x * x, axis=-1, keepdims=True)
    h = (x * lax.rsqrt(ms + EPS) * nw_ref[...]).astype(BF16)

    def seg(i):
        return jnp.dot(h, w_ref[:, i * SEG:(i + 1) * SEG], preferred_element_type=F32)

    def group_norm(t, w):
        ss = jnp.dot((t * t).astype(BF16), g_ref[...], preferred_element_type=F32)
        return (t * lax.rsqrt(ss + EPS) * w).astype(BF16)

    qa_ref[...] = group_norm(seg(0), wq_ref[...])
    ka_ref[...] = group_norm(seg(1), wk_ref[...])

    v = seg(2).astype(BF16)
    tm = v.shape[0]
    lane = lax.broadcasted_iota(jnp.int32, (tm, LANES), 1)
    ones_col = jnp.where(lane == 0, 1.0, 0.0).astype(BF16)
    for hh in range(A_HEADS):
        va_ref[:, hh * V_AUG:hh * V_AUG + V_DIM] = v[:, hh * V_DIM:(hh + 1) * V_DIM]
        va_ref[:, hh * V_AUG + V_DIM:(hh + 1) * V_AUG] = ones_col

    qh = seg(3)
    qh_ref[...] = (qh * jax.nn.sigmoid(qh) * (HGRN_DK ** -0.5)).astype(BF16)
    ffw_ref[...] = seg(4)
    fbw_ref[...] = seg(5)
    ih_ref[...] = seg(6).astype(BF16)
    gh_ref[...] = seg(7)


def _proj_call(x2d, p):
    t = x2d.shape[0]
    tm = TM_PROJ
    assert t % tm == 0
    row = lambda i: (i, 0)
    const = lambda i: (0, 0)
    out_shape = (
        jax.ShapeDtypeStruct((t, SEG), BF16),
        jax.ShapeDtypeStruct((t, SEG), BF16),
        jax.ShapeDtypeStruct((t, A_HEADS * V_AUG), BF16),
        jax.ShapeDtypeStruct((t, SEG), BF16),
        jax.ShapeDtypeStruct((t, SEG), F32),
        jax.ShapeDtypeStruct((t, SEG), F32),
        jax.ShapeDtypeStruct((t, SEG), BF16),
        jax.ShapeDtypeStruct((t, SEG), F32),
    )
    out_specs = tuple(pl.BlockSpec((tm, s.shape[1]), row) for s in out_shape)
    return pl.pallas_call(
        _proj_kernel,
        out_shape=out_shape,
        grid=(t // tm,),
        in_specs=[
            pl.BlockSpec((tm, D_MODEL), row),
            pl.BlockSpec((1, D_MODEL), const),
            pl.BlockSpec((D_MODEL, N_SEG * SEG), const),
            pl.BlockSpec((SEG, SEG), const),
            pl.BlockSpec((1, SEG), const),
            pl.BlockSpec((1, SEG), const),
        ],
        out_specs=out_specs,
        compiler_params=pltpu.CompilerParams(
            dimension_semantics=("arbitrary",), vmem_limit_bytes=VMEM_LIMIT),
        name="proj",
    )(x2d, p["attn_norm_w"], p["w_in"], p["group_mat"], p["wq"], p["wk"])


def _bias_kernel(tab_ref, bucket_ref, o_ref):
    hh = pl.program_id(0)
    n_d, tq, _ = bucket_ref.shape
    rows = 64
    per_d = tq // rows

    def body(i, carry):
        d = i // per_d
        r0 = pl.multiple_of((i % per_d) * rows, rows)
        bucket = bucket_ref[d, pl.ds(r0, rows), :]
        out = jnp.zeros(bucket.shape, F32)
        for bb in range(N_BUCKETS):
            out = jnp.where(bucket == bb, tab_ref[bb, hh], out)
        o_ref[0, d, pl.ds(r0, rows), :] = out
        return carry

    lax.fori_loop(0, n_d * per_d, body, 0)


def _bias_call(table, buckets):
    n_d, tq, tk = buckets.shape
    return pl.pallas_call(
        _bias_kernel,
        out_shape=jax.ShapeDtypeStruct((A_HEADS, n_d, tq, tk), F32),
        grid=(A_HEADS,),
        in_specs=[
            pl.BlockSpec(memory_space=pltpu.SMEM),
            pl.BlockSpec((n_d, tq, tk), lambda hi: (0, 0, 0)),
        ],
        out_specs=pl.BlockSpec((1, n_d, tq, tk), lambda hi: (hi, 0, 0, 0)),
        compiler_params=pltpu.CompilerParams(
            dimension_semantics=("arbitrary",), vmem_limit_bytes=VMEM_LIMIT),
        name="bias_tiles",
    )(table, buckets)


def _attn_kernel(fast_ref, lam_ref, q_ref, k_ref, v_ref, bias_ref, sw_ref, o_ref,
                 m_ref, acc_ref, *, nk):
    qi = pl.program_id(2)
    tq = q_ref.shape[1]
    tk = bias_ref.shape[3]

    q = q_ref[0]
    lane = lax.broadcasted_iota(jnp.int32, q.shape, 1)
    zero = jnp.zeros_like(q)
    qs = jnp.concatenate([jnp.where(lane < QK_DIM, q, zero),
                          jnp.where(lane >= QK_DIM, q, zero)], axis=0)

    def scores(j):
        k = k_ref[0, pl.ds(pl.multiple_of(j * tk, tk), tk), :]
        return lax.dot_general(qs, k, (((1,), (1,)), ((), ())), preferred_element_type=F32)

    def values(j):
        return v_ref[0, pl.ds(pl.multiple_of(j * tk, tk), tk), :]

    def add_bias(s, bias):
        return (s.reshape(2, tq, tk) + bias[None]).reshape(2 * tq, tk)

    def for_tiles(lo, hi, fn):
        def body(j, carry):
            fn(j)
            return carry
        lax.fori_loop(lo, hi, body, 0)

    def biased_scores(j):
        d_idx = jnp.clip(j - qi, -BAND, BAND) + BAND
        return add_bias(scores(j), bias_ref[0, d_idx])

    acc_ref[...] = jnp.zeros(acc_ref.shape, F32)

    @pl.when(fast_ref[0] == 1)
    def _():
        def group(g):
            for u in range(KEY_GROUP):
                j = g * KEY_GROUP + u
                p = jnp.exp2(biased_scores(j)).astype(BF16)
                acc_ref[...] += jnp.dot(p, values(j), preferred_element_type=F32)

        for_tiles(0, nk // KEY_GROUP, group)

    @pl.when(fast_ref[0] != 1)
    def _():
        m_ref[...] = jnp.full(m_ref.shape, NEG_BIG, F32)

        def update(j):
            s = biased_scores(j)
            m_prev = m_ref[...]
            m_new = jnp.maximum(m_prev, jnp.max(s, axis=-1, keepdims=True))
            alpha = jnp.exp2(m_prev - m_new)
            p = jnp.exp2(s - m_new).astype(BF16)
            acc_ref[...] = alpha * acc_ref[...] + jnp.dot(p, values(j), preferred_element_type=F32)
            m_ref[...] = m_new

        for_tiles(0, nk, update)

    acc = acc_ref[...]
    o1 = acc[:tq, :V_DIM] / acc[:tq, V_DIM:V_DIM + 1]
    o2 = acc[tq:, :V_DIM] / acc[tq:, V_DIM:V_DIM + 1]
    o = o1 - lam_ref[0] * o2
    ms = jnp.mean(o * o, axis=-1, keepdims=True)
    o = o * lax.rsqrt(ms + EPS) * sw_ref[...] * (1.0 - LAM_INIT)
    o_ref[0] = o.astype(BF16)


def _attn_call(qa, ka, va, p):
    b, s, _ = qa.shape
    tq, tk = TQ, TK
    assert tq == tk and s % tq == 0
    nq, nk = s // tq, s // tk
    assert nk % KEY_GROUP == 0
    n_d = 2 * BAND + 1
    once = pl.Buffered(1)
    grid_spec = pltpu.PrefetchScalarGridSpec(
        num_scalar_prefetch=2,
        grid=(b, A_HEADS, nq),
        in_specs=[
            pl.BlockSpec((1, tq, LANES), lambda bi, hi, qi, *_: (bi, qi, hi)),
            pl.BlockSpec((1, s, LANES), lambda bi, hi, qi, *_: (bi, 0, hi), pipeline_mode=once),
            pl.BlockSpec((1, s, V_AUG), lambda bi, hi, qi, *_: (bi, 0, hi), pipeline_mode=once),
            pl.BlockSpec((1, n_d, tq, tk), lambda bi, hi, qi, *_: (hi, 0, 0, 0), pipeline_mode=once),
            pl.BlockSpec((1, V_DIM), lambda bi, hi, qi, *_: (0, 0)),
        ],
        out_specs=pl.BlockSpec((1, tq, V_DIM), lambda bi, hi, qi, *_: (bi, qi, hi)),
        scratch_shapes=[pltpu.VMEM((2 * tq, 1), F32), pltpu.VMEM((2 * tq, V_AUG), F32)],
    )
    return pl.pallas_call(
        functools.partial(_attn_kernel, nk=nk),
        out_shape=jax.ShapeDtypeStruct((b, s, A_HEADS * V_DIM), BF16),
        grid_spec=grid_spec,
        compiler_params=pltpu.CompilerParams(
            dimension_semantics=("arbitrary", "arbitrary", "arbitrary"),
            vmem_limit_bytes=VMEM_LIMIT),
        name="diff_attn",
    )(p["fast"], p["lam"], qa, ka, va, p["bias_tiles"], p["subln_w"])


def _hgrn_kernel(qf_ref, ff_ref, vf_ref, qb_ref, fb_ref, vb_ref, lb_ref, of_ref, ob_ref,
                 sf_ref, sb_ref):
    @pl.when(pl.program_id(2) == 0)
    def _():
        sf_ref[...] = jnp.zeros(sf_ref.shape, F32)
        sb_ref[...] = jnp.zeros(sb_ref.shape, F32)

    n_chunks = qf_ref.shape[1] // CHUNK
    r = lax.broadcasted_iota(jnp.int32, (CHUNK, CHUNK), 0)
    c = lax.broadcasted_iota(jnp.int32, (CHUNK, CHUNK), 1)
    tril = r >= c
    triu = r <= c
    nt = (((1,), (1,)), ((), ()))
    tn = (((0,), (0,)), ((), ()))

    def chunk(q_ref, f_ref, v_ref, o_ref, st_ref, lb, rows, mask, last_row):
        q = q_ref[0, rows, :].astype(F32)
        v = v_ref[0, rows, :]
        f = lb + (1.0 - lb) * jax.nn.sigmoid(f_ref[0, rows, :])
        k = 1.0 - f
        lf = jnp.log(f)
        lf_hi = lf.astype(BF16)
        lf_lo = (lf - lf_hi.astype(F32)).astype(BF16)
        ones = jnp.where(mask, 1.0, 0.0).astype(BF16)
        bcum = (jnp.dot(ones, lf_hi, preferred_element_type=F32)
                + jnp.dot(ones, lf_lo, preferred_element_type=F32))
        b_last = bcum[last_row:last_row + 1, :]
        q_t = (q * jnp.exp(bcum)).astype(BF16)
        k_t = (k * jnp.exp(-bcum)).astype(BF16)
        a = lax.dot_general(q_t, k_t, nt, preferred_element_type=F32)
        a = jnp.where(mask, a, 0.0).astype(BF16)
        st = st_ref[...]
        o = (jnp.dot(a, v, preferred_element_type=F32)
             + lax.dot_general(q_t, st.astype(BF16), nt, preferred_element_type=F32))
        o_ref[0, rows, :] = o
        k_d = (k * jnp.exp(b_last - bcum)).astype(BF16)
        ds = lax.dot_general(v, k_d, tn, preferred_element_type=F32)
        st_ref[...] = jnp.exp(b_last) * st + ds

    lb_f = lb_ref[0:1, :]
    lb_b = lb_ref[1:2, :]

    def body(i, carry):
        rows_f = pl.ds(pl.multiple_of(i * CHUNK, CHUNK), CHUNK)
        chunk(qf_ref, ff_ref, vf_ref, of_ref, sf_ref, lb_f, rows_f, tril, CHUNK - 1)
        rows_b = pl.ds(pl.multiple_of((n_chunks - 1 - i) * CHUNK, CHUNK), CHUNK)
        chunk(qb_ref, fb_ref, vb_ref, ob_ref, sb_ref, lb_b, rows_b, triu, 0)
        return carry

    lax.fori_loop(0, n_chunks, body, 0)


def _hgrn_call(qh, ffw, fbw, ih, p):
    b, s, _ = qh.shape
    sb = min(SB_HGRN, s)
    assert s % sb == 0 and sb % CHUNK == 0
    nsb = s // sb
    fwd = lambda bi, hi, j: (bi, j, hi)
    bwd = lambda bi, hi, j: (bi, nsb - 1 - j, hi)
    blk = (1, sb, LANES)
    return pl.pallas_call(
        _hgrn_kernel,
        out_shape=(jax.ShapeDtypeStruct((b, s, SEG), F32), jax.ShapeDtypeStruct((b, s, SEG), F32)),
        grid=(b, HGRN_HEADS, nsb),
        in_specs=[
            pl.BlockSpec(blk, fwd), pl.BlockSpec(blk, fwd), pl.BlockSpec(blk, fwd),
            pl.BlockSpec(blk, bwd), pl.BlockSpec(blk, bwd), pl.BlockSpec(blk, bwd),
            pl.BlockSpec((2, LANES), lambda bi, hi, j: (0, hi)),
        ],
        out_specs=(pl.BlockSpec(blk, fwd), pl.BlockSpec(blk, bwd)),
        scratch_shapes=[pltpu.VMEM((HGRN_DV, HGRN_DK), F32), pltpu.VMEM((HGRN_DV, HGRN_DK), F32)],
        compiler_params=pltpu.CompilerParams(
            dimension_semantics=("arbitrary", "arbitrary", "arbitrary"),
            vmem_limit_bytes=VMEM_LIMIT),
        name="hgrn2",
    )(qh, ffw, ih, qh, fbw, ih, p["lb"])


def _mlp_kernel(x_ref, oa_ref, of_ref, ob_ref, g_ref, hw_ref, wo_ref, mw_ref, w1_ref, w2_ref, y_ref):
    o = of_ref[...] + ob_ref[...]
    g = g_ref[...]
    parts = [oa_ref[...]]
    for hh in range(HGRN_HEADS):
        sl = slice(hh * HGRN_DV, (hh + 1) * HGRN_DV)
        oh = o[:, sl]
        ms = jnp.mean(oh * oh, axis=-1, keepdims=True)
        gh = g[:, sl]
        parts.append((oh * lax.rsqrt(ms + EPS) * hw_ref[...] * (gh * jax.nn.sigmoid(gh))).astype(BF16))
    cat = jnp.concatenate(parts, axis=1)
    x1 = x_ref[...] + jnp.dot(cat, wo_ref[...], preferred_element_type=F32)
    ms = jnp.mean(x1 * x1, axis=-1, keepdims=True)
    h2 = (x1 * lax.rsqrt(ms + EPS) * mw_ref[...]).astype(BF16)
    acc = x1
    for cc in range(D_FF // D_MODEL):
        sl = slice(cc * D_MODEL, (cc + 1) * D_MODEL)
        u = jnp.dot(h2, w1_ref[:, sl], preferred_element_type=F32)
        a = jnp.square(jnp.maximum(u, 0.0)).astype(BF16)
        acc = acc + jnp.dot(a, w2_ref[sl, :], preferred_element_type=F32)
    y_ref[...] = acc


def _mlp_call(x2d, oa, o_f, o_b, gh, p):
    t = x2d.shape[0]
    tm = TM_MLP
    assert t % tm == 0
    row = lambda i: (i, 0)
    const = lambda i: (0, 0)
    once = pl.Buffered(1)
    return pl.pallas_call(
        _mlp_kernel,
        out_shape=jax.ShapeDtypeStruct((t, D_MODEL), F32),
        grid=(t // tm,),
        in_specs=[
            pl.BlockSpec((tm, D_MODEL), row),
            pl.BlockSpec((tm, SEG), row),
            pl.BlockSpec((tm, SEG), row),
            pl.BlockSpec((tm, SEG), row),
            pl.BlockSpec((tm, SEG), row),
            pl.BlockSpec((1, HGRN_DV), const),
            pl.BlockSpec((D_MODEL, D_MODEL), const, pipeline_mode=once),
            pl.BlockSpec((1, D_MODEL), const),
            pl.BlockSpec((D_MODEL, D_FF), const, pipeline_mode=once),
            pl.BlockSpec((D_FF, D_MODEL), const, pipeline_mode=once),
        ],
        out_specs=pl.BlockSpec((tm, D_MODEL), row),
        compiler_params=pltpu.CompilerParams(
            dimension_semantics=("arbitrary",), vmem_limit_bytes=VMEM_LIMIT),
        name="out_mlp",
    )(x2d, oa, o_f, o_b, gh, p["hgrn_norm_w"], p["w_out"], p["mlp_norm_w"], p["w_mlp_in"], p["w_mlp_out"])


def _prepare(attn_norm_w, w_in, qk_norm_w, diff_lambda, diff_subln_w, rel_bias, hgrn_lb,
             hgrn_norm_w, w_out, mlp_norm_w, w_mlp_in, w_mlp_out):
    l = 0
    lam_p = diff_lambda[l].astype(F32)
    lam = jnp.exp(jnp.sum(lam_p[0] * lam_p[1])) - jnp.exp(jnp.sum(lam_p[2] * lam_p[3])) + LAM_INIT
    lb = jnp.cumsum(jax.nn.softmax(hgrn_lb.astype(F32), axis=1), axis=1)[:, l]

    rb = rel_bias.astype(F32)
    left = jnp.take(rb, _t5_bucket(jnp.array([-BAND * TK])), axis=0)
    table = (rb - left) * LOG2E
    ii = jnp.arange(TQ)[:, None]
    jj = jnp.arange(TK)[None, :]
    buckets = _t5_bucket(jnp.stack([d * TK + jj - ii for d in range(-BAND, BAND + 1)]))
    tiles = _bias_call(table, buckets.astype(jnp.int32))

    wq_eff = qk_norm_w[l, 0].astype(F32) * (QK_DIM ** -0.5 * LOG2E)
    wk_eff = qk_norm_w[l, 1].astype(F32)
    logit_bound = 1.02 * QK_DIM * jnp.max(jnp.abs(wq_eff * wk_eff)) + 2.0 * jnp.max(jnp.abs(table))
    fast = (logit_bound <= FAST_EXP_RANGE).astype(jnp.int32).reshape(1)

    reps = SEG // QK_DIM
    grp = jnp.arange(SEG) // QK_DIM
    group_mat = jnp.where(grp[:, None] == grp[None, :], 1.0 / QK_DIM, 0.0).astype(BF16)
    return dict(
        attn_norm_w=attn_norm_w[l][None].astype(F32),
        w_in=w_in[l].astype(BF16),
        group_mat=group_mat,
        wq=jnp.tile(wq_eff, reps)[None],
        wk=jnp.tile(wk_eff, reps)[None],
        fast=fast,
        lam=jnp.reshape(lam, (1,)).astype(F32),
        bias_tiles=tiles,
        subln_w=diff_subln_w[l][None].astype(F32),
        lb=lb,
        hgrn_norm_w=hgrn_norm_w[l][None].astype(F32),
        w_out=w_out[l].astype(BF16),
        mlp_norm_w=mlp_norm_w[l][None].astype(F32),
        w_mlp_in=w_mlp_in[l].astype(BF16),
        w_mlp_out=w_mlp_out[l].astype(BF16),
    )


def _layer(x, p):
    b, s, d = x.shape
    x2d = x.reshape(b * s, d)
    qa, ka, va, qh, ffw, fbw, ih, gh = _proj_call(x2d, p)
    r3 = lambda t: t.reshape(b, s, t.shape[-1])
    oa = _attn_call(r3(qa), r3(ka), r3(va), p)
    o_f, o_b = _hgrn_call(r3(qh), r3(ffw), r3(fbw), r3(ih), p)
    y = _mlp_call(x2d, oa.reshape(b * s, -1), o_f.reshape(b * s, -1), o_b.reshape(b * s, -1), gh, p)
    return y.reshape(b, s, d)


def kernel(x_prompt, x_sample, attn_norm_w, w_in, qk_norm_w, diff_lambda, diff_subln_w, rel_bias,
           hgrn_lb, hgrn_norm_w, w_out, mlp_norm_w, w_mlp_in, w_mlp_out):
    p = _prepare(attn_norm_w, w_in, qk_norm_w, diff_lambda, diff_subln_w, rel_bias, hgrn_lb,
                 hgrn_norm_w, w_out, mlp_norm_w, w_mlp_in, w_mlp_out)
    return _layer(x_prompt, p), _layer(x_sample, p)
```

```python
import functools
import math

import jax
import jax.numpy as jnp
from jax import lax
from jax.experimental import pallas as pl
from jax.experimental.pallas import tpu as pltpu

F32 = jnp.float32
BF16 = jnp.bfloat16

D_MODEL = 1024
A_HEADS = 4
V_DIM = 128
QK_DIM = 64
HGRN_HEADS = 4
HGRN_DK = 128
HGRN_DV = 128
SEG = 512
N_SEG = 8
D_FF = 4 * D_MODEL
N_BUCKETS = 32
MAX_DISTANCE = 128
CHUNK = 64
EPS = 1e-6
LOG2E = math.log2(math.e)
LAM_INIT = 0.8 - 0.6 * math.exp(-0.3 * 0)

LANES = 128
V_AUG = 2 * LANES
VMEM_LIMIT = 56 * 1024 * 1024

TM_PROJ = 512
TM_MLP = 512
TQ = 512
TK = 512
KEY_GROUP = 8
BAND = 2
SB_HGRN = 2048
HGRN_GROUP = 8
NEG_BIG = -1e30
FAST_EXP_RANGE = 60.0


def _t5_bucket(rel):
    nb = N_BUCKETS // 2
    max_exact = nb // 2
    ret = jnp.where(rel > 0, nb, 0)
    n = jnp.abs(rel)
    nf = jnp.maximum(n, 1).astype(jnp.float32)
    large = max_exact + (jnp.log(nf / max_exact) / math.log(MAX_DISTANCE / max_exact)
                         * (nb - max_exact)).astype(jnp.int32)
    large = jnp.minimum(large, nb - 1)
    return ret + jnp.where(n < max_exact, n, large)


def _proj_kernel(x_ref, nw_ref, w_ref, g_ref, wq_ref, wk_ref,
                 qa_ref, ka_ref, va_ref, qh_ref, ffw_ref, fbw_ref, ih_ref, gh_ref):
    x = x_ref[...]
    ms = jnp.mean(x * x, axis=-1, keepdims=True)
    h = (x * lax.rsqrt(ms + EPS) * nw_ref[...]).astype(BF16)

    def seg(i):
        return jnp.dot(h, w_ref[:, i * SEG:(i + 1) * SEG], preferred_element_type=F32)

    def group_norm(t, w):
        ss = jnp.dot((t * t).astype(BF16), g_ref[...], preferred_element_type=F32)
        return (t * lax.rsqrt(ss + EPS) * w).astype(BF16)

    qa_ref[...] = group_norm(seg(0), wq_ref[...])
    ka_ref[...] = group_norm(seg(1), wk_ref[...])

    v = seg(2).astype(BF16)
    tm = v.shape[0]
    lane = lax.broadcasted_iota(jnp.int32, (tm, LANES), 1)
    ones_col = jnp.where(lane == 0, 1.0, 0.0).astype(BF16)
    for hh in range(A_HEADS):
        va_ref[:, hh * V_AUG:hh * V_AUG + V_DIM] = v[:, hh * V_DIM:(hh + 1) * V_DIM]
        va_ref[:, hh * V_AUG + V_DIM:(hh + 1) * V_AUG] = ones_col

    qh = seg(3)
    qh_ref[...] = (qh * jax.nn.sigmoid(qh) * (HGRN_DK ** -0.5)).astype(BF16)
    ffw_ref[...] = seg(4)
    fbw_ref[...] = seg(5)
    ih_ref[...] = seg(6).astype(BF16)
    gh_ref[...] = seg(7)


def _proj_call(x2d, p):
    t = x2d.shape[0]
    tm = TM_PROJ
    assert t % tm == 0
    row = lambda i: (i, 0)
    const = lambda i: (0, 0)
    out_shape = (
        jax.ShapeDtypeStruct((t, SEG), BF16),
        jax.ShapeDtypeStruct((t, SEG), BF16),
        jax.ShapeDtypeStruct((t, A_HEADS * V_AUG), BF16),
        jax.ShapeDtypeStruct((t, SEG), BF16),
        jax.ShapeDtypeStruct((t, SEG), F32),
        jax.ShapeDtypeStruct((t, SEG), F32),
        jax.ShapeDtypeStruct((t, SEG), BF16),
        jax.ShapeDtypeStruct((t, SEG), F32),
    )
    out_specs = tuple(pl.BlockSpec((tm, s.shape[1]), row) for s in out_shape)
    return pl.pallas_call(
        _proj_kernel,
        out_shape=out_shape,
        grid=(t // tm,),
        in_specs=[
            pl.BlockSpec((tm, D_MODEL), row),
            pl.BlockSpec((1, D_MODEL), const),
            pl.BlockSpec((D_MODEL, N_SEG * SEG), const),
            pl.BlockSpec((SEG, SEG), const),
            pl.BlockSpec((1, SEG), const),
            pl.BlockSpec((1, SEG), const),
        ],
        out_specs=out_specs,
        compiler_params=pltpu.CompilerParams(
            dimension_semantics=("arbitrary",), vmem_limit_bytes=VMEM_LIMIT),
        name="proj",
    )(x2d, p["attn_norm_w"], p["w_in"], p["group_mat"], p["wq"], p["wk"])


def _bias_kernel(tab_ref, bucket_ref, o_ref):
    hh = pl.program_id(0)
    n_d, tq, _ = bucket_ref.shape
    rows = 64
    per_d = tq // rows

    def body(i, carry):
        d = i // per_d
        r0 = pl.multiple_of((i % per_d) * rows, rows)
        bucket = bucket_ref[d, pl.ds(r0, rows), :]
        out = jnp.zeros(bucket.shape, F32)
        for bb in range(N_BUCKETS):
            out = jnp.where(bucket == bb, tab_ref[bb, hh], out)
        o_ref[0, d, pl.ds(r0, rows), :] = out
        return carry

    lax.fori_loop(0, n_d * per_d, body, 0)


def _bias_call(table, buckets):
    n_d, tq, tk = buckets.shape
    return pl.pallas_call(
        _bias_kernel,
        out_shape=jax.ShapeDtypeStruct((A_HEADS, n_d, tq, tk), F32),
        grid=(A_HEADS,),
        in_specs=[
            pl.BlockSpec(memory_space=pltpu.SMEM),
            pl.BlockSpec((n_d, tq, tk), lambda hi: (0, 0, 0)),
        ],
        out_specs=pl.BlockSpec((1, n_d, tq, tk), lambda hi: (hi, 0, 0, 0)),
        compiler_params=pltpu.CompilerParams(
            dimension_semantics=("arbitrary",), vmem_limit_bytes=VMEM_LIMIT),
        name="bias_tiles",
    )(table, buckets)


def _attn_kernel(fast_ref, lam_ref, q_ref, k_ref, v_ref, bias_ref, sw_ref, o_ref,
                 m_ref, acc_ref, *, nk):
    qi = pl.program_id(2)
    tq = q_ref.shape[1]
    tk = bias_ref.shape[3]

    q = q_ref[0]
    lane = lax.broadcasted_iota(jnp.int32, q.shape, 1)
    zero = jnp.zeros_like(q)
    qs = jnp.concatenate([jnp.where(lane < QK_DIM, q, zero),
                          jnp.where(lane >= QK_DIM, q, zero)], axis=0)

    def scores(j):
        k = k_ref[0, pl.ds(pl.multiple_of(j * tk, tk), tk), :]
        return lax.dot_general(qs, k, (((1,), (1,)), ((), ())), preferred_element_type=F32)

    def values(j):
        return v_ref[0, pl.ds(pl.multiple_of(j * tk, tk), tk), :]

    def add_bias(s, bias):
        return (s.reshape(2, tq, tk) + bias[None]).reshape(2 * tq, tk)

    def for_tiles(lo, hi, fn):
        def body(j, carry):
            fn(j)
            return carry
        lax.fori_loop(lo, hi, body, 0)

    def biased_scores(j):
        d_idx = jnp.clip(j - qi, -BAND, BAND) + BAND
        return add_bias(scores(j), bias_ref[0, d_idx])

    acc_ref[...] = jnp.zeros(acc_ref.shape, F32)

    @pl.when(fast_ref[0] == 1)
    def _():
        def group(g):
            for u in range(KEY_GROUP):
                j = g * KEY_GROUP + u
                p = jnp.exp2(biased_scores(j)).astype(BF16)
                acc_ref[...] += jnp.dot(p, values(j), preferred_element_type=F32)

        for_tiles(0, nk // KEY_GROUP, group)

    @pl.when(fast_ref[0] != 1)
    def _():
        m_ref[...] = jnp.full(m_ref.shape, NEG_BIG, F32)

        def update(j):
            s = biased_scores(j)
            m_prev = m_ref[...]
            m_new = jnp.maximum(m_prev, jnp.max(s, axis=-1, keepdims=True))
            alpha = jnp.exp2(m_prev - m_new)
            p = jnp.exp2(s - m_new).astype(BF16)
            acc_ref[...] = alpha * acc_ref[...] + jnp.dot(p, values(j), preferred_element_type=F32)
            m_ref[...] = m_new

        for_tiles(0, nk, update)

    acc = acc_ref[...]
    o1 = acc[:tq, :V_DIM] / acc[:tq, V_DIM:V_DIM + 1]
    o2 = acc[tq:, :V_DIM] / acc[tq:, V_DIM:V_DIM + 1]
    o = o1 - lam_ref[0] * o2
    ms = jnp.mean(o * o, axis=-1, keepdims=True)
    o = o * lax.rsqrt(ms + EPS) * sw_ref[...] * (1.0 - LAM_INIT)
    o_ref[0] = o.astype(BF16)


def _attn_call(qa, ka, va, p):
    b, s, _ = qa.shape
    tq, tk = TQ, TK
    assert tq == tk and s % tq == 0
    nq, nk = s // tq, s // tk
    assert nk % KEY_GROUP == 0
    n_d = 2 * BAND + 1
    once = pl.Buffered(1)
    grid_spec = pltpu.PrefetchScalarGridSpec(
        num_scalar_prefetch=2,
        grid=(b, A_HEADS, nq),
        in_specs=[
            pl.BlockSpec((1, tq, LANES), lambda bi, hi, qi, *_: (bi, qi, hi)),
            pl.BlockSpec((1, s, LANES), lambda bi, hi, qi, *_: (bi, 0, hi), pipeline_mode=once),
            pl.BlockSpec((1, s, V_AUG), lambda bi, hi, qi, *_: (bi, 0, hi), pipeline_mode=once),
            pl.BlockSpec((1, n_d, tq, tk), lambda bi, hi, qi, *_: (hi, 0, 0, 0), pipeline_mode=once),
            pl.BlockSpec((1, V_DIM), lambda bi, hi, qi, *_: (0, 0)),
        ],
        out_specs=pl.BlockSpec((1, tq, V_DIM), lambda bi, hi, qi, *_: (bi, qi, hi)),
        scratch_shapes=[pltpu.VMEM((2 * tq, 1), F32), pltpu.VMEM((2 * tq, V_AUG), F32)],
    )
    return pl.pallas_call(
        functools.partial(_attn_kernel, nk=nk),
        out_shape=jax.ShapeDtypeStruct((b, s, A_HEADS * V_DIM), BF16),
        grid_spec=grid_spec,
        compiler_params=pltpu.CompilerParams(
            dimension_semantics=("arbitrary", "arbitrary", "arbitrary"),
            vmem_limit_bytes=VMEM_LIMIT),
        name="diff_attn",
    )(p["fast"], p["lam"], qa, ka, va, p["bias_tiles"], p["subln_w"])


def _hgrn_kernel(qf_ref, ff_ref, vf_ref, qb_ref, fb_ref, vb_ref, lb_ref, of_ref, ob_ref,
                 sf_ref, sb_ref):
    @pl.when(pl.program_id(2) == 0)
    def _():
        sf_ref[...] = jnp.zeros(sf_ref.shape, F32)
        sb_ref[...] = jnp.zeros(sb_ref.shape, F32)

    g_chunks = HGRN_GROUP
    rows_g = g_chunks * CHUNK
    n_groups = qf_ref.shape[1] // rows_g
    r = lax.broadcasted_iota(jnp.int32, (CHUNK, CHUNK), 0)
    c = lax.broadcasted_iota(jnp.int32, (CHUNK, CHUNK), 1)
    pos = lax.broadcasted_iota(jnp.int32, (rows_g, LANES), 0) % CHUNK
    nt = (((1,), (1,)), ((), ()))
    tn = (((0,), (0,)), ((), ()))

    def group(q_ref, f_ref, v_ref, o_ref, st, lb, row0, reverse):
        rows = pl.ds(row0, rows_g)
        q = q_ref[0, rows, :].astype(F32)
        v = v_ref[0, rows, :]
        f = lb + (1.0 - lb) * jax.nn.sigmoid(f_ref[0, rows, :])
        k = 1.0 - f
        b = jnp.log(f)
        step = 1
        while step < CHUNK:
            if reverse:
                b = b + jnp.where(pos < CHUNK - step, pltpu.roll(b, rows_g - step, 0), 0.0)
            else:
                b = b + jnp.where(pos >= step, pltpu.roll(b, step, 0), 0.0)
            step *= 2
        last_row = 0 if reverse else CHUNK - 1
        mask = (r <= c) if reverse else (r >= c)
        b3 = b.reshape(g_chunks, CHUNK, LANES)
        b_last = b3[:, last_row:last_row + 1, :]
        decay = jnp.exp(b_last)
        q_t = (q * jnp.exp(b)).astype(BF16).reshape(g_chunks, CHUNK, LANES)
        k_t = (k * jnp.exp(-b)).astype(BF16).reshape(g_chunks, CHUNK, LANES)
        k_d = (k.reshape(g_chunks, CHUNK, LANES) * jnp.exp(b_last - b3)).astype(BF16)
        v3 = v.reshape(g_chunks, CHUNK, LANES)
        a = [jnp.where(mask, lax.dot_general(q_t[g], k_t[g], nt, preferred_element_type=F32),
                       0.0).astype(BF16) for g in range(g_chunks)]
        ds = [lax.dot_general(v3[g], k_d[g], tn, preferred_element_type=F32)
              for g in range(g_chunks)]
        o_intra = [jnp.dot(a[g], v3[g], preferred_element_type=F32) for g in range(g_chunks)]
        outs = [None] * g_chunks
        for g in (range(g_chunks - 1, -1, -1) if reverse else range(g_chunks)):
            outs[g] = o_intra[g] + lax.dot_general(q_t[g], st.astype(BF16), nt,
                                                   preferred_element_type=F32)
            st = decay[g] * st + ds[g]
        o_ref[0, rows, :] = jnp.concatenate(outs, axis=0)
        return st

    lb_f = lb_ref[0:1, :]
    lb_b = lb_ref[1:2, :]

    def body(i, carry):
        row_f = pl.multiple_of(i * rows_g, rows_g)
        sf_ref[...] = group(qf_ref, ff_ref, vf_ref, of_ref, sf_ref[...], lb_f, row_f, False)
        row_b = pl.multiple_of((n_groups - 1 - i) * rows_g, rows_g)
        sb_ref[...] = group(qb_ref, fb_ref, vb_ref, ob_ref, sb_ref[...], lb_b, row_b, True)
        return carry

    lax.fori_loop(0, n_groups, body, 0)


def _hgrn_call(qh, ffw, fbw, ih, p):
    b, s, _ = qh.shape
    sb = min(SB_HGRN, s)
    assert s % sb == 0 and sb % (HGRN_GROUP * CHUNK) == 0
    nsb = s // sb
    fwd = lambda bi, hi, j: (bi, j, hi)
    bwd = lambda bi, hi, j: (bi, nsb - 1 - j, hi)
    blk = (1, sb, LANES)
    return pl.pallas_call(
        _hgrn_kernel,
        out_shape=(jax.ShapeDtypeStruct((b, s, SEG), F32), jax.ShapeDtypeStruct((b, s, SEG), F32)),
        grid=(b, HGRN_HEADS, nsb),
        in_specs=[
            pl.BlockSpec(blk, fwd), pl.BlockSpec(blk, fwd), pl.BlockSpec(blk, fwd),
            pl.BlockSpec(blk, bwd), pl.BlockSpec(blk, bwd), pl.BlockSpec(blk, bwd),
            pl.BlockSpec((2, LANES), lambda bi, hi, j: (0, hi)),
        ],
        out_specs=(pl.BlockSpec(blk, fwd), pl.BlockSpec(blk, bwd)),
        scratch_shapes=[pltpu.VMEM((HGRN_DV, HGRN_DK), F32), pltpu.VMEM((HGRN_DV, HGRN_DK), F32)],
        compiler_params=pltpu.CompilerParams(
            dimension_semantics=("arbitrary", "arbitrary", "arbitrary"),
            vmem_limit_bytes=VMEM_LIMIT),
        name="hgrn2",
    )(qh, ffw, ih, qh, fbw, ih, p["lb"])


def _mlp_kernel(x_ref, oa_ref, of_ref, ob_ref, g_ref, hw_ref, wo_ref, mw_ref, w1_ref, w2_ref, y_ref):
    o = of_ref[...] + ob_ref[...]
    g = g_ref[...]
    parts = [oa_ref[...]]
    for hh in range(HGRN_HEADS):
        sl = slice(hh * HGRN_DV, (hh + 1) * HGRN_DV)
        oh = o[:, sl]
        ms = jnp.mean(oh * oh, axis=-1, keepdims=True)
        gh = g[:, sl]
        parts.append((oh * lax.rsqrt(ms + EPS) * hw_ref[...] * (gh * jax.nn.sigmoid(gh))).astype(BF16))
    cat = jnp.concatenate(parts, axis=1)
    x1 = x_ref[...] + jnp.dot(cat, wo_ref[...], preferred_element_type=F32)
    ms = jnp.mean(x1 * x1, axis=-1, keepdims=True)
    h2 = (x1 * lax.rsqrt(ms + EPS) * mw_ref[...]).astype(BF16)
    acc = x1
    for cc in range(D_FF // D_MODEL):
        sl = slice(cc * D_MODEL, (cc + 1) * D_MODEL)
        u = jnp.dot(h2, w1_ref[:, sl], preferred_element_type=F32)
        a = jnp.square(jnp.maximum(u, 0.0)).astype(BF16)
        acc = acc + jnp.dot(a, w2_ref[sl, :], preferred_element_type=F32)
    y_ref[...] = acc


def _mlp_call(x2d, oa, o_f, o_b, gh, p):
    t = x2d.shape[0]
    tm = TM_MLP
    assert t % tm == 0
    row = lambda i: (i, 0)
    const = lambda i: (0, 0)
    once = pl.Buffered(1)
    return pl.pallas_call(
        _mlp_kernel,
        out_shape=jax.ShapeDtypeStruct((t, D_MODEL), F32),
        grid=(t // tm,),
        in_specs=[
            pl.BlockSpec((tm, D_MODEL), row),
            pl.BlockSpec((tm, SEG), row),
            pl.BlockSpec((tm, SEG), row),
            pl.BlockSpec((tm, SEG), row),
            pl.BlockSpec((tm, SEG), row),
            pl.BlockSpec((1, HGRN_DV), const),
            pl.BlockSpec((D_MODEL, D_MODEL), const, pipeline_mode=once),
            pl.BlockSpec((1, D_MODEL), const),
            pl.BlockSpec((D_MODEL, D_FF), const, pipeline_mode=once),
            pl.BlockSpec((D_FF, D_MODEL), const, pipeline_mode=once),
        ],
        out_specs=pl.BlockSpec((tm, D_MODEL), row),
        compiler_params=pltpu.CompilerParams(
            dimension_semantics=("arbitrary",), vmem_limit_bytes=VMEM_LIMIT),
        name="out_mlp",
    )(x2d, oa, o_f, o_b, gh, p["hgrn_norm_w"], p["w_out"], p["mlp_norm_w"], p["w_mlp_in"], p["w_mlp_out"])


def _prepare(attn_norm_w, w_in, qk_norm_w, diff_lambda, diff_subln_w, rel_bias, hgrn_lb,
             hgrn_norm_w, w_out, mlp_norm_w, w_mlp_in, w_mlp_out):
    l = 0
    lam_p = diff_lambda[l].astype(F32)
    lam = jnp.exp(jnp.sum(lam_p[0] * lam_p[1])) - jnp.exp(jnp.sum(lam_p[2] * lam_p[3])) + LAM_INIT
    lb = jnp.cumsum(jax.nn.softmax(hgrn_lb.astype(F32), axis=1), axis=1)[:, l]

    rb = rel_bias.astype(F32)
    left = jnp.take(rb, _t5_bucket(jnp.array([-BAND * TK])), axis=0)
    table = (rb - left) * LOG2E
    ii = jnp.arange(TQ)[:, None]
    jj = jnp.arange(TK)[None, :]
    buckets = _t5_bucket(jnp.stack([d * TK + jj - ii for d in range(-BAND, BAND + 1)]))
    tiles = _bias_call(table, buckets.astype(jnp.int32))

    wq_eff = qk_norm_w[l, 0].astype(F32) * (QK_DIM ** -0.5 * LOG2E)
    wk_eff = qk_norm_w[l, 1].astype(F32)
    logit_bound = 1.02 * QK_DIM * jnp.max(jnp.abs(wq_eff * wk_eff)) + 2.0 * jnp.max(jnp.abs(table))
    fast = (logit_bound <= FAST_EXP_RANGE).astype(jnp.int32).reshape(1)

    reps = SEG // QK_DIM
    grp = jnp.arange(SEG) // QK_DIM
    group_mat = jnp.where(grp[:, None] == grp[None, :], 1.0 / QK_DIM, 0.0).astype(BF16)
    return dict(
        attn_norm_w=attn_norm_w[l][None].astype(F32),
        w_in=w_in[l].astype(BF16),
        group_mat=group_mat,
        wq=jnp.tile(wq_eff, reps)[None],
        wk=jnp.tile(wk_eff, reps)[None],
        fast=fast,
        lam=jnp.reshape(lam, (1,)).astype(F32),
        bias_tiles=tiles,
        subln_w=diff_subln_w[l][None].astype(F32),
        lb=lb,
        hgrn_norm_w=hgrn_norm_w[l][None].astype(F32),
        w_out=w_out[l].astype(BF16),
        mlp_norm_w=mlp_norm_w[l][None].astype(F32),
        w_mlp_in=w_mlp_in[l].astype(BF16),
        w_mlp_out=w_mlp_out[l].astype(BF16),
    )


def _layer(x, p):
    b, s, d = x.shape
    x2d = x.reshape(b * s, d)
    qa, ka, va, qh, ffw, fbw, ih, gh = _proj_call(x2d, p)
    r3 = lambda t: t.reshape(b, s, t.shape[-1])
    oa = _attn_call(r3(qa), r3(ka), r3(va), p)
    o_f, o_b = _hgrn_call(r3(qh), r3(ffw), r3(fbw), r3(ih), p)
    y = _mlp_call(x2d, oa.reshape(b * s, -1), o_f.reshape(b * s, -1), o_b.reshape(b * s, -1), gh, p)
    return y.reshape(b, s, d)


def kernel(x_prompt, x_sample, attn_norm_w, w_in, qk_norm_w, diff_lambda, diff_subln_w, rel_bias,
           hgrn_lb, hgrn_norm_w, w_out, mlp_norm_w, w_mlp_in, w_mlp_out):
    p = _prepare(attn_norm_w, w_in, qk_norm_w, diff_lambda, diff_subln_w, rel_bias, hgrn_lb,
                 hgrn_norm_w, w_out, mlp_norm_w, w_mlp_in, w_mlp_out)
    return _layer(x_prompt, p), _layer(x_sample, p)
```

```python
import functools
import math

import jax
import jax.numpy as jnp
from jax import lax
from jax.experimental import pallas as pl
from jax.experimental.pallas import tpu as pltpu

F32 = jnp.float32
BF16 = jnp.bfloat16

D_MODEL = 1024
A_HEADS = 4
V_DIM = 128
QK_DIM = 64
HGRN_HEADS = 4
HGRN_DK = 128
HGRN_DV = 128
SEG = 512
N_SEG = 8
D_FF = 4 * D_MODEL
N_BUCKETS = 32
MAX_DISTANCE = 128
CHUNK = 64
EPS = 1e-6
LOG2E = math.log2(math.e)
LAM_INIT = 0.8 - 0.6 * math.exp(-0.3 * 0)

LANES = 128
V_AUG = 2 * LANES
VMEM_LIMIT = 56 * 1024 * 1024

TM_PROJ = 512
TM_MLP = 512
TQ = 1024
TK = 512
KEY_GROUP = 8
BAND_OFFSETS = tuple(range(-2, TQ // TK + 2))
SB_HGRN = 2048
HGRN_GROUP = 8
NEG_BIG = -1e30
FAST_EXP_RANGE = 60.0


def _silu(x):
    return (0.5 * x) * (1.0 + jnp.tanh(0.5 * x))


def _t5_bucket(rel):
    nb = N_BUCKETS // 2
    max_exact = nb // 2
    ret = jnp.where(rel > 0, nb, 0)
    n = jnp.abs(rel)
    nf = jnp.maximum(n, 1).astype(jnp.float32)
    large = max_exact + (jnp.log(nf / max_exact) / math.log(MAX_DISTANCE / max_exact)
                         * (nb - max_exact)).astype(jnp.int32)
    large = jnp.minimum(large, nb - 1)
    return ret + jnp.where(n < max_exact, n, large)


def _proj_kernel(x_ref, nw_ref, w_ref, g_ref, wq_ref, wk_ref,
                 qa_ref, ka_ref, va_ref, qh_ref, ffw_ref, fbw_ref, ih_ref, gh_ref):
    x = x_ref[...]
    ms = jnp.mean(x * x, axis=-1, keepdims=True)
    h = (x * lax.rsqrt(ms + EPS) * nw_ref[...]).astype(BF16)

    def seg(i):
        return jnp.dot(h, w_ref[:, i * SEG:(i + 1) * SEG], preferred_element_type=F32)

    def group_norm(t, w):
        ss = jnp.dot((t * t).astype(BF16), g_ref[...], preferred_element_type=F32)
        return (t * lax.rsqrt(ss + EPS) * w).astype(BF16)

    qa_ref[...] = group_norm(seg(0), wq_ref[...])
    ka_ref[...] = group_norm(seg(1), wk_ref[...])

    v = seg(2).astype(BF16)
    tm = v.shape[0]
    lane = lax.broadcasted_iota(jnp.int32, (tm, LANES), 1)
    ones_col = jnp.where(lane == 0, 1.0, 0.0).astype(BF16)
    for hh in range(A_HEADS):
        va_ref[:, hh * V_AUG:hh * V_AUG + V_DIM] = v[:, hh * V_DIM:(hh + 1) * V_DIM]
        va_ref[:, hh * V_AUG + V_DIM:(hh + 1) * V_AUG] = ones_col

    qh = seg(3)
    qh_ref[...] = (_silu(qh) * (HGRN_DK ** -0.5)).astype(BF16)
    ffw_ref[...] = seg(4)
    fbw_ref[...] = seg(5)
    ih_ref[...] = seg(6).astype(BF16)
    gh_ref[...] = seg(7)


def _proj_call(x2d, p):
    t = x2d.shape[0]
    tm = TM_PROJ
    assert t % tm == 0
    row = lambda i: (i, 0)
    const = lambda i: (0, 0)
    out_shape = (
        jax.ShapeDtypeStruct((t, SEG), BF16),
        jax.ShapeDtypeStruct((t, SEG), BF16),
        jax.ShapeDtypeStruct((t, A_HEADS * V_AUG), BF16),
        jax.ShapeDtypeStruct((t, SEG), BF16),
        jax.ShapeDtypeStruct((t, SEG), F32),
        jax.ShapeDtypeStruct((t, SEG), F32),
        jax.ShapeDtypeStruct((t, SEG), BF16),
        jax.ShapeDtypeStruct((t, SEG), F32),
    )
    out_specs = tuple(pl.BlockSpec((tm, s.shape[1]), row) for s in out_shape)
    return pl.pallas_call(
        _proj_kernel,
        out_shape=out_shape,
        grid=(t // tm,),
        in_specs=[
            pl.BlockSpec((tm, D_MODEL), row),
            pl.BlockSpec((1, D_MODEL), const),
            pl.BlockSpec((D_MODEL, N_SEG * SEG), const),
            pl.BlockSpec((SEG, SEG), const),
            pl.BlockSpec((1, SEG), const),
            pl.BlockSpec((1, SEG), const),
        ],
        out_specs=out_specs,
        compiler_params=pltpu.CompilerParams(
            dimension_semantics=("arbitrary",), vmem_limit_bytes=VMEM_LIMIT),
        name="proj",
    )(x2d, p["attn_norm_w"], p["w_in"], p["group_mat"], p["wq"], p["wk"])


def _bias_kernel(tab_ref, bucket_ref, o_ref):
    hh = pl.program_id(0)
    n_d, tq, _ = bucket_ref.shape
    rows = 64
    per_d = tq // rows

    def body(i, carry):
        d = i // per_d
        r0 = pl.multiple_of((i % per_d) * rows, rows)
        bucket = bucket_ref[d, pl.ds(r0, rows), :]
        out = jnp.zeros(bucket.shape, F32)
        for bb in range(N_BUCKETS):
            out = jnp.where(bucket == bb, tab_ref[bb, hh], out)
        o_ref[0, d, pl.ds(r0, rows), :] = out
        return carry

    lax.fori_loop(0, n_d * per_d, body, 0)


def _bias_call(table, buckets):
    n_d, tq, tk = buckets.shape
    return pl.pallas_call(
        _bias_kernel,
        out_shape=jax.ShapeDtypeStruct((A_HEADS, n_d, tq, tk), F32),
        grid=(A_HEADS,),
        in_specs=[
            pl.BlockSpec(memory_space=pltpu.SMEM),
            pl.BlockSpec((n_d, tq, tk), lambda hi: (0, 0, 0)),
        ],
        out_specs=pl.BlockSpec((1, n_d, tq, tk), lambda hi: (hi, 0, 0, 0)),
        compiler_params=pltpu.CompilerParams(
            dimension_semantics=("arbitrary",), vmem_limit_bytes=VMEM_LIMIT),
        name="bias_tiles",
    )(table, buckets)


def _attn_kernel(fast_ref, lam_ref, q_ref, k_ref, v_ref, bias_ref, sw_ref, o_ref,
                 m_ref, acc_ref, *, nk):
    qi = pl.program_id(2)
    tq = q_ref.shape[1]
    tk = bias_ref.shape[3]

    q = q_ref[0]
    lane = lax.broadcasted_iota(jnp.int32, q.shape, 1)
    zero = jnp.zeros_like(q)
    qs = jnp.concatenate([jnp.where(lane < QK_DIM, q, zero),
                          jnp.where(lane >= QK_DIM, q, zero)], axis=0)

    def scores(j):
        k = k_ref[0, pl.ds(pl.multiple_of(j * tk, tk), tk), :]
        return lax.dot_general(qs, k, (((1,), (1,)), ((), ())), preferred_element_type=F32)

    def values(j):
        return v_ref[0, pl.ds(pl.multiple_of(j * tk, tk), tk), :]

    def add_bias(s, bias):
        return (s.reshape(2, tq, tk) + bias[None]).reshape(2 * tq, tk)

    def for_tiles(lo, hi, fn):
        def body(j, carry):
            fn(j)
            return carry
        lax.fori_loop(lo, hi, body, 0)

    def biased_scores(j):
        d_idx = jnp.clip(j - qi * (tq // tk), BAND_OFFSETS[0], BAND_OFFSETS[-1]) - BAND_OFFSETS[0]
        return add_bias(scores(j), bias_ref[0, d_idx])

    acc_ref[...] = jnp.zeros(acc_ref.shape, F32)

    @pl.when(fast_ref[0] == 1)
    def _():
        def group(g):
            for u in range(KEY_GROUP):
                j = g * KEY_GROUP + u
                p = jnp.exp2(biased_scores(j)).astype(BF16)
                acc_ref[...] += jnp.dot(p, values(j), preferred_element_type=F32)

        for_tiles(0, nk // KEY_GROUP, group)

    @pl.when(fast_ref[0] != 1)
    def _():
        m_ref[...] = jnp.full(m_ref.shape, NEG_BIG, F32)

        def update(j):
            s = biased_scores(j)
            m_prev = m_ref[...]
            m_new = jnp.maximum(m_prev, jnp.max(s, axis=-1, keepdims=True))
            alpha = jnp.exp2(m_prev - m_new)
            p = jnp.exp2(s - m_new).astype(BF16)
            acc_ref[...] = alpha * acc_ref[...] + jnp.dot(p, values(j), preferred_element_type=F32)
            m_ref[...] = m_new

        for_tiles(0, nk, update)

    acc = acc_ref[...]
    o1 = acc[:tq, :V_DIM] / acc[:tq, V_DIM:V_DIM + 1]
    o2 = acc[tq:, :V_DIM] / acc[tq:, V_DIM:V_DIM + 1]
    o = o1 - lam_ref[0] * o2
    ms = jnp.mean(o * o, axis=-1, keepdims=True)
    o = o * lax.rsqrt(ms + EPS) * sw_ref[...] * (1.0 - LAM_INIT)
    o_ref[0] = o.astype(BF16)


def _attn_call(qa, ka, va, p):
    b, s, _ = qa.shape
    tq, tk = TQ, TK
    assert tq % tk == 0 and s % tq == 0
    nq, nk = s // tq, s // tk
    assert nk % KEY_GROUP == 0
    n_d = len(BAND_OFFSETS)
    once = pl.Buffered(1)
    grid_spec = pltpu.PrefetchScalarGridSpec(
        num_scalar_prefetch=2,
        grid=(b, A_HEADS, nq),
        in_specs=[
            pl.BlockSpec((1, tq, LANES), lambda bi, hi, qi, *_: (bi, qi, hi)),
            pl.BlockSpec((1, s, LANES), lambda bi, hi, qi, *_: (bi, 0, hi), pipeline_mode=once),
            pl.BlockSpec((1, s, V_AUG), lambda bi, hi, qi, *_: (bi, 0, hi), pipeline_mode=once),
            pl.BlockSpec((1, n_d, tq, tk), lambda bi, hi, qi, *_: (hi, 0, 0, 0), pipeline_mode=once),
            pl.BlockSpec((1, V_DIM), lambda bi, hi, qi, *_: (0, 0)),
        ],
        out_specs=pl.BlockSpec((1, tq, V_DIM), lambda bi, hi, qi, *_: (bi, qi, hi)),
        scratch_shapes=[pltpu.VMEM((2 * tq, 1), F32), pltpu.VMEM((2 * tq, V_AUG), F32)],
    )
    return pl.pallas_call(
        functools.partial(_attn_kernel, nk=nk),
        out_shape=jax.ShapeDtypeStruct((b, s, A_HEADS * V_DIM), BF16),
        grid_spec=grid_spec,
        compiler_params=pltpu.CompilerParams(
            dimension_semantics=("arbitrary", "arbitrary", "arbitrary"),
            vmem_limit_bytes=VMEM_LIMIT),
        name="diff_attn",
    )(p["fast"], p["lam"], qa, ka, va, p["bias_tiles"], p["subln_w"])


def _hgrn_kernel(qf_ref, ff_ref, vf_ref, qb_ref, fb_ref, vb_ref, lb_ref, of_ref, ob_ref,
                 sf_ref, sb_ref):
    @pl.when(pl.program_id(2) == 0)
    def _():
        sf_ref[...] = jnp.zeros(sf_ref.shape, F32)
        sb_ref[...] = jnp.zeros(sb_ref.shape, F32)

    g_chunks = HGRN_GROUP
    rows_g = g_chunks * CHUNK
    n_groups = qf_ref.shape[1] // rows_g
    r = lax.broadcasted_iota(jnp.int32, (CHUNK, CHUNK), 0)
    c = lax.broadcasted_iota(jnp.int32, (CHUNK, CHUNK), 1)
    pos = lax.broadcasted_iota(jnp.int32, (rows_g, LANES), 0) % CHUNK
    nt = (((1,), (1,)), ((), ()))
    tn = (((0,), (0,)), ((), ()))

    def group(q_ref, f_ref, v_ref, o_ref, st, lb, row0, reverse):
        rows = pl.ds(row0, rows_g)
        q = q_ref[0, rows, :].astype(F32)
        v = v_ref[0, rows, :]
        t = jnp.tanh(0.5 * f_ref[0, rows, :])
        half = 0.5 * (1.0 - lb)
        f = (lb + half) + half * t
        k = half * (1.0 - t)
        b = jnp.log(f)
        step = 1
        while step < CHUNK:
            if reverse:
                b = b + jnp.where(pos < CHUNK - step, pltpu.roll(b, rows_g - step, 0), 0.0)
            else:
                b = b + jnp.where(pos >= step, pltpu.roll(b, step, 0), 0.0)
            step *= 2
        last_row = 0 if reverse else CHUNK - 1
        mask = (r <= c) if reverse else (r >= c)
        b3 = b.reshape(g_chunks, CHUNK, LANES)
        b_last = b3[:, last_row:last_row + 1, :]
        decay = jnp.exp(b_last)
        q_t = (q * jnp.exp(b)).astype(BF16).reshape(g_chunks, CHUNK, LANES)
        k_t = (k * jnp.exp(-b)).astype(BF16).reshape(g_chunks, CHUNK, LANES)
        k_d = (k.reshape(g_chunks, CHUNK, LANES) * jnp.exp(b_last - b3)).astype(BF16)
        v3 = v.reshape(g_chunks, CHUNK, LANES)
        a = [jnp.where(mask, lax.dot_general(q_t[g], k_t[g], nt, preferred_element_type=F32),
                       0.0).astype(BF16) for g in range(g_chunks)]
        ds = [lax.dot_general(v3[g], k_d[g], tn, preferred_element_type=F32)
              for g in range(g_chunks)]
        o_intra = [jnp.dot(a[g], v3[g], preferred_element_type=F32) for g in range(g_chunks)]
        outs = [None] * g_chunks
        for g in (range(g_chunks - 1, -1, -1) if reverse else range(g_chunks)):
            outs[g] = o_intra[g] + lax.dot_general(q_t[g], st.astype(BF16), nt,
                                                   preferred_element_type=F32)
            st = decay[g] * st + ds[g]
        o_ref[0, rows, :] = jnp.concatenate(outs, axis=0)
        return st

    lb_f = lb_ref[0:1, :]
    lb_b = lb_ref[1:2, :]

    def body(i, carry):
        row_f = pl.multiple_of(i * rows_g, rows_g)
        sf_ref[...] = group(qf_ref, ff_ref, vf_ref, of_ref, sf_ref[...], lb_f, row_f, False)
        row_b = pl.multiple_of((n_groups - 1 - i) * rows_g, rows_g)
        sb_ref[...] = group(qb_ref, fb_ref, vb_ref, ob_ref, sb_ref[...], lb_b, row_b, True)
        return carry

    lax.fori_loop(0, n_groups, body, 0)


def _hgrn_call(qh, ffw, fbw, ih, p):
    b, s, _ = qh.shape
    sb = min(SB_HGRN, s)
    assert s % sb == 0 and sb % (HGRN_GROUP * CHUNK) == 0
    nsb = s // sb
    fwd = lambda bi, hi, j: (bi, j, hi)
    bwd = lambda bi, hi, j: (bi, nsb - 1 - j, hi)
    blk = (1, sb, LANES)
    return pl.pallas_call(
        _hgrn_kernel,
        out_shape=(jax.ShapeDtypeStruct((b, s, SEG), F32), jax.ShapeDtypeStruct((b, s, SEG), F32)),
        grid=(b, HGRN_HEADS, nsb),
        in_specs=[
            pl.BlockSpec(blk, fwd), pl.BlockSpec(blk, fwd), pl.BlockSpec(blk, fwd),
            pl.BlockSpec(blk, bwd), pl.BlockSpec(blk, bwd), pl.BlockSpec(blk, bwd),
            pl.BlockSpec((2, LANES), lambda bi, hi, j: (0, hi)),
        ],
        out_specs=(pl.BlockSpec(blk, fwd), pl.BlockSpec(blk, bwd)),
        scratch_shapes=[pltpu.VMEM((HGRN_DV, HGRN_DK), F32), pltpu.VMEM((HGRN_DV, HGRN_DK), F32)],
        compiler_params=pltpu.CompilerParams(
            dimension_semantics=("arbitrary", "arbitrary", "arbitrary"),
            vmem_limit_bytes=VMEM_LIMIT),
        name="hgrn2",
    )(qh, ffw, ih, qh, fbw, ih, p["lb"])


def _mlp_kernel(x_ref, oa_ref, of_ref, ob_ref, g_ref, hw_ref, wo_ref, mw_ref, w1_ref, w2_ref, y_ref):
    o = of_ref[...] + ob_ref[...]
    g = g_ref[...]
    parts = [oa_ref[...]]
    for hh in range(HGRN_HEADS):
        sl = slice(hh * HGRN_DV, (hh + 1) * HGRN_DV)
        oh = o[:, sl]
        ms = jnp.mean(oh * oh, axis=-1, keepdims=True)
        gh = g[:, sl]
        parts.append((oh * lax.rsqrt(ms + EPS) * hw_ref[...] * _silu(gh)).astype(BF16))
    cat = jnp.concatenate(parts, axis=1)
    x1 = x_ref[...] + jnp.dot(cat, wo_ref[...], preferred_element_type=F32)
    ms = jnp.mean(x1 * x1, axis=-1, keepdims=True)
    h2 = (x1 * lax.rsqrt(ms + EPS) * mw_ref[...]).astype(BF16)
    acc = x1
    for cc in range(D_FF // D_MODEL):
        sl = slice(cc * D_MODEL, (cc + 1) * D_MODEL)
        u = jnp.dot(h2, w1_ref[:, sl], preferred_element_type=F32)
        a = jnp.square(jnp.maximum(u, 0.0)).astype(BF16)
        acc = acc + jnp.dot(a, w2_ref[sl, :], preferred_element_type=F32)
    y_ref[...] = acc


def _mlp_call(x2d, oa, o_f, o_b, gh, p):
    t = x2d.shape[0]
    tm = TM_MLP
    assert t % tm == 0
    row = lambda i: (i, 0)
    const = lambda i: (0, 0)
    once = pl.Buffered(1)
    return pl.pallas_call(
        _mlp_kernel,
        out_shape=jax.ShapeDtypeStruct((t, D_MODEL), F32),
        grid=(t // tm,),
        in_specs=[
            pl.BlockSpec((tm, D_MODEL), row),
            pl.BlockSpec((tm, SEG), row),
            pl.BlockSpec((tm, SEG), row),
            pl.BlockSpec((tm, SEG), row),
            pl.BlockSpec((tm, SEG), row),
            pl.BlockSpec((1, HGRN_DV), const),
            pl.BlockSpec((D_MODEL, D_MODEL), const, pipeline_mode=once),
            pl.BlockSpec((1, D_MODEL), const),
            pl.BlockSpec((D_MODEL, D_FF), const, pipeline_mode=once),
            pl.BlockSpec((D_FF, D_MODEL), const, pipeline_mode=once),
        ],
        out_specs=pl.BlockSpec((tm, D_MODEL), row),
        compiler_params=pltpu.CompilerParams(
            dimension_semantics=("arbitrary",), vmem_limit_bytes=VMEM_LIMIT),
        name="out_mlp",
    )(x2d, oa, o_f, o_b, gh, p["hgrn_norm_w"], p["w_out"], p["mlp_norm_w"], p["w_mlp_in"], p["w_mlp_out"])


def _prepare(attn_norm_w, w_in, qk_norm_w, diff_lambda, diff_subln_w, rel_bias, hgrn_lb,
             hgrn_norm_w, w_out, mlp_norm_w, w_mlp_in, w_mlp_out):
    l = 0
    lam_p = diff_lambda[l].astype(F32)
    lam = jnp.exp(jnp.sum(lam_p[0] * lam_p[1])) - jnp.exp(jnp.sum(lam_p[2] * lam_p[3])) + LAM_INIT
    lb = jnp.cumsum(jax.nn.softmax(hgrn_lb.astype(F32), axis=1), axis=1)[:, l]

    rb = rel_bias.astype(F32)
    left = jnp.take(rb, _t5_bucket(jnp.array([BAND_OFFSETS[0] * TK])), axis=0)
    table = (rb - left) * LOG2E
    ii = jnp.arange(TQ)[:, None]
    jj = jnp.arange(TK)[None, :]
    buckets = _t5_bucket(jnp.stack([d * TK + jj - ii for d in BAND_OFFSETS]))
    tiles = _bias_call(table, buckets.astype(jnp.int32))

    wq_eff = qk_norm_w[l, 0].astype(F32) * (QK_DIM ** -0.5 * LOG2E)
    wk_eff = qk_norm_w[l, 1].astype(F32)
    logit_bound = 1.02 * QK_DIM * jnp.max(jnp.abs(wq_eff * wk_eff)) + 2.0 * jnp.max(jnp.abs(table))
    fast = (logit_bound <= FAST_EXP_RANGE).astype(jnp.int32).reshape(1)

    reps = SEG // QK_DIM
    grp = jnp.arange(SEG) // QK_DIM
    group_mat = jnp.where(grp[:, None] == grp[None, :], 1.0 / QK_DIM, 0.0).astype(BF16)
    return dict(
        attn_norm_w=attn_norm_w[l][None].astype(F32),
        w_in=w_in[l].astype(BF16),
        group_mat=group_mat,
        wq=jnp.tile(wq_eff, reps)[None],
        wk=jnp.tile(wk_eff, reps)[None],
        fast=fast,
        lam=jnp.reshape(lam, (1,)).astype(F32),
        bias_tiles=tiles,
        subln_w=diff_subln_w[l][None].astype(F32),
        lb=lb,
        hgrn_norm_w=hgrn_norm_w[l][None].astype(F32),
        w_out=w_out[l].astype(BF16),
        mlp_norm_w=mlp_norm_w[l][None].astype(F32),
        w_mlp_in=w_mlp_in[l].astype(BF16),
        w_mlp_out=w_mlp_out[l].astype(BF16),
    )


def _layer(x, p):
    b, s, d = x.shape
    x2d = x.reshape(b * s, d)
    qa, ka, va, qh, ffw, fbw, ih, gh = _proj_call(x2d, p)
    r3 = lambda t: t.reshape(b, s, t.shape[-1])
    oa = _attn_call(r3(qa), r3(ka), r3(va), p)
    o_f, o_b = _hgrn_call(r3(qh), r3(ffw), r3(fbw), r3(ih), p)
    y = _mlp_call(x2d, oa.reshape(b * s, -1), o_f.reshape(b * s, -1), o_b.reshape(b * s, -1), gh, p)
    return y.reshape(b, s, d)


def kernel(x_prompt, x_sample, attn_norm_w, w_in, qk_norm_w, diff_lambda, diff_subln_w, rel_bias,
           hgrn_lb, hgrn_norm_w, w_out, mlp_norm_w, w_mlp_in, w_mlp_out):
    p = _prepare(attn_norm_w, w_in, qk_norm_w, diff_lambda, diff_subln_w, rel_bias, hgrn_lb,
                 hgrn_norm_w, w_out, mlp_norm_w, w_mlp_in, w_mlp_out)
    return _layer(x_prompt, p), _layer(x_sample, p)
```

```python
import functools
import math

import jax
import jax.numpy as jnp
from jax import lax
from jax.experimental import pallas as pl
from jax.experimental.pallas import tpu as pltpu

F32 = jnp.float32
BF16 = jnp.bfloat16

D_MODEL = 1024
A_HEADS = 4
V_DIM = 128
QK_DIM = 64
HGRN_HEADS = 4
HGRN_DK = 128
HGRN_DV = 128
SEG = 512
N_SEG = 8
D_FF = 4 * D_MODEL
N_BUCKETS = 32
MAX_DISTANCE = 128
CHUNK = 64
EPS = 1e-6
LOG2E = math.log2(math.e)
LAM_INIT = 0.8 - 0.6 * math.exp(-0.3 * 0)

LANES = 128
V_AUG = 2 * LANES
VMEM_LIMIT = 56 * 1024 * 1024

TM_PROJ = 512
TM_MLP = 512
TQ = 1024
TK = 512
KEY_GROUP = 8
BAND_OFFSETS = tuple(range(-2, TQ // TK + 2))
NEG_BIG = -1e30
FAST_EXP_RANGE = 60.0


def _silu(x):
    return (0.5 * x) * (1.0 + jnp.tanh(0.5 * x))


def _t5_bucket(rel):
    nb = N_BUCKETS // 2
    max_exact = nb // 2
    ret = jnp.where(rel > 0, nb, 0)
    n = jnp.abs(rel)
    nf = jnp.maximum(n, 1).astype(jnp.float32)
    large = max_exact + (jnp.log(nf / max_exact) / math.log(MAX_DISTANCE / max_exact)
                         * (nb - max_exact)).astype(jnp.int32)
    large = jnp.minimum(large, nb - 1)
    return ret + jnp.where(n < max_exact, n, large)


def _proj_kernel(x_ref, nw_ref, w_ref, g_ref, wq_ref, wk_ref,
                 qa_ref, ka_ref, va_ref, qh_ref, ffw_ref, fbw_ref, ih_ref, gh_ref):
    x = x_ref[...]
    ms = jnp.mean(x * x, axis=-1, keepdims=True)
    h = (x * lax.rsqrt(ms + EPS) * nw_ref[...]).astype(BF16)

    def seg(i):
        return jnp.dot(h, w_ref[:, i * SEG:(i + 1) * SEG], preferred_element_type=F32)

    def group_norm(t, w):
        ss = jnp.dot((t * t).astype(BF16), g_ref[...], preferred_element_type=F32)
        return (t * lax.rsqrt(ss + EPS) * w).astype(BF16)

    qa_ref[...] = group_norm(seg(0), wq_ref[...])
    ka_ref[...] = group_norm(seg(1), wk_ref[...])

    v = seg(2).astype(BF16)
    tm = v.shape[0]
    lane = lax.broadcasted_iota(jnp.int32, (tm, LANES), 1)
    ones_col = jnp.where(lane == 0, 1.0, 0.0).astype(BF16)
    for hh in range(A_HEADS):
        va_ref[:, hh * V_AUG:hh * V_AUG + V_DIM] = v[:, hh * V_DIM:(hh + 1) * V_DIM]
        va_ref[:, hh * V_AUG + V_DIM:(hh + 1) * V_AUG] = ones_col

    qh = seg(3)
    qh_ref[...] = (_silu(qh) * (HGRN_DK ** -0.5)).astype(BF16)
    ffw_ref[...] = seg(4)
    fbw_ref[...] = seg(5)
    ih_ref[...] = seg(6).astype(BF16)
    gh_ref[...] = seg(7)


def _proj_call(x2d, p):
    t = x2d.shape[0]
    tm = TM_PROJ
    assert t % tm == 0
    row = lambda i: (i, 0)
    const = lambda i: (0, 0)
    out_shape = (
        jax.ShapeDtypeStruct((t, SEG), BF16),
        jax.ShapeDtypeStruct((t, SEG), BF16),
        jax.ShapeDtypeStruct((t, A_HEADS * V_AUG), BF16),
        jax.ShapeDtypeStruct((t, SEG), BF16),
        jax.ShapeDtypeStruct((t, SEG), F32),
        jax.ShapeDtypeStruct((t, SEG), F32),
        jax.ShapeDtypeStruct((t, SEG), BF16),
        jax.ShapeDtypeStruct((t, SEG), F32),
    )
    out_specs = tuple(pl.BlockSpec((tm, s.shape[1]), row) for s in out_shape)
    return pl.pallas_call(
        _proj_kernel,
        out_shape=out_shape,
        grid=(t // tm,),
        in_specs=[
            pl.BlockSpec((tm, D_MODEL), row),
            pl.BlockSpec((1, D_MODEL), const),
            pl.BlockSpec((D_MODEL, N_SEG * SEG), const),
            pl.BlockSpec((SEG, SEG), const),
            pl.BlockSpec((1, SEG), const),
            pl.BlockSpec((1, SEG), const),
        ],
        out_specs=out_specs,
        compiler_params=pltpu.CompilerParams(
            dimension_semantics=("arbitrary",), vmem_limit_bytes=VMEM_LIMIT),
        name="proj",
    )(x2d, p["attn_norm_w"], p["w_in"], p["group_mat"], p["wq"], p["wk"])


def _bias_kernel(tab_ref, bucket_ref, o_ref):
    hh = pl.program_id(0)
    n_d, tq, _ = bucket_ref.shape
    rows = 64
    per_d = tq // rows

    def body(i, carry):
        d = i // per_d
        r0 = pl.multiple_of((i % per_d) * rows, rows)
        bucket = bucket_ref[d, pl.ds(r0, rows), :]
        out = jnp.zeros(bucket.shape, F32)
        for bb in range(N_BUCKETS):
            out = jnp.where(bucket == bb, tab_ref[bb, hh], out)
        o_ref[0, d, pl.ds(r0, rows), :] = out
        return carry

    lax.fori_loop(0, n_d * per_d, body, 0)


def _bias_call(table, buckets):
    n_d, tq, tk = buckets.shape
    return pl.pallas_call(
        _bias_kernel,
        out_shape=jax.ShapeDtypeStruct((A_HEADS, n_d, tq, tk), F32),
        grid=(A_HEADS,),
        in_specs=[
            pl.BlockSpec(memory_space=pltpu.SMEM),
            pl.BlockSpec((n_d, tq, tk), lambda hi: (0, 0, 0)),
        ],
        out_specs=pl.BlockSpec((1, n_d, tq, tk), lambda hi: (hi, 0, 0, 0)),
        compiler_params=pltpu.CompilerParams(
            dimension_semantics=("arbitrary",), vmem_limit_bytes=VMEM_LIMIT),
        name="bias_tiles",
    )(table, buckets)


def _mixer_kernel(fast_ref, lam_ref, q_ref, k_ref, v_ref, bias_ref, sw_ref,
                  hqf_ref, hff_ref, hvf_ref, hqb_ref, hfb_ref, hvb_ref, lb_ref,
                  o_ref, of_ref, ob_ref, m_ref, acc_ref, sf_ref, sb_ref, *, nk):
    qi = pl.program_id(2)
    tq = q_ref.shape[1]
    tk = bias_ref.shape[3]
    n_trips = nk // KEY_GROUP
    g_chunks = tq // (CHUNK * n_trips)
    rows_g = g_chunks * CHUNK

    @pl.when(qi == 0)
    def _():
        sf_ref[...] = jnp.zeros(sf_ref.shape, F32)
        sb_ref[...] = jnp.zeros(sb_ref.shape, F32)

    lb_f = lb_ref[0:1, :]
    lb_b = lb_ref[1:2, :]

    def hgrn_step(g):
        row_f = pl.multiple_of(g * rows_g, rows_g)
        sf_ref[...] = _hgrn_group(hqf_ref, hff_ref, hvf_ref, of_ref, sf_ref[...], lb_f, row_f,
                                  g_chunks, False)
        row_b = pl.multiple_of((n_trips - 1 - g) * rows_g, rows_g)
        sb_ref[...] = _hgrn_group(hqb_ref, hfb_ref, hvb_ref, ob_ref, sb_ref[...], lb_b, row_b,
                                  g_chunks, True)

    q = q_ref[0]
    lane = lax.broadcasted_iota(jnp.int32, q.shape, 1)
    zero = jnp.zeros_like(q)
    qs = jnp.concatenate([jnp.where(lane < QK_DIM, q, zero),
                          jnp.where(lane >= QK_DIM, q, zero)], axis=0)

    def scores(j):
        k = k_ref[0, pl.ds(pl.multiple_of(j * tk, tk), tk), :]
        return lax.dot_general(qs, k, (((1,), (1,)), ((), ())), preferred_element_type=F32)

    def values(j):
        return v_ref[0, pl.ds(pl.multiple_of(j * tk, tk), tk), :]

    def add_bias(s, bias):
        return (s.reshape(2, tq, tk) + bias[None]).reshape(2 * tq, tk)

    def for_tiles(lo, hi, fn):
        def body(j, carry):
            fn(j)
            return carry
        lax.fori_loop(lo, hi, body, 0)

    def biased_scores(j):
        d_idx = jnp.clip(j - qi * (tq // tk), BAND_OFFSETS[0], BAND_OFFSETS[-1]) - BAND_OFFSETS[0]
        return add_bias(scores(j), bias_ref[0, d_idx])

    acc_ref[...] = jnp.zeros(acc_ref.shape, F32)

    @pl.when(fast_ref[0] == 1)
    def _():
        def group(g):
            for u in range(KEY_GROUP):
                j = g * KEY_GROUP + u
                p = jnp.exp2(biased_scores(j)).astype(BF16)
                acc_ref[...] += jnp.dot(p, values(j), preferred_element_type=F32)
            hgrn_step(g)

        for_tiles(0, n_trips, group)

    @pl.when(fast_ref[0] != 1)
    def _():
        m_ref[...] = jnp.full(m_ref.shape, NEG_BIG, F32)

        def update(j):
            s = biased_scores(j)
            m_prev = m_ref[...]
            m_new = jnp.maximum(m_prev, jnp.max(s, axis=-1, keepdims=True))
            alpha = jnp.exp2(m_prev - m_new)
            p = jnp.exp2(s - m_new).astype(BF16)
            acc_ref[...] = alpha * acc_ref[...] + jnp.dot(p, values(j), preferred_element_type=F32)
            m_ref[...] = m_new

        for_tiles(0, n_trips, hgrn_step)
        for_tiles(0, nk, update)

    acc = acc_ref[...]
    o1 = acc[:tq, :V_DIM] / acc[:tq, V_DIM:V_DIM + 1]
    o2 = acc[tq:, :V_DIM] / acc[tq:, V_DIM:V_DIM + 1]
    o = o1 - lam_ref[0] * o2
    ms = jnp.mean(o * o, axis=-1, keepdims=True)
    o = o * lax.rsqrt(ms + EPS) * sw_ref[...] * (1.0 - LAM_INIT)
    o_ref[0] = o.astype(BF16)


def _mixer_call(qa, ka, va, qh, ffw, fbw, ih, p):
    b, s, _ = qa.shape
    tq, tk = TQ, TK
    assert tq % tk == 0 and s % tq == 0 and A_HEADS == HGRN_HEADS
    nq, nk = s // tq, s // tk
    assert nk % KEY_GROUP == 0 and tq % (CHUNK * (nk // KEY_GROUP)) == 0
    n_d = len(BAND_OFFSETS)
    once = pl.Buffered(1)
    fwd = lambda bi, hi, qi, *_: (bi, qi, hi)
    bwd = lambda bi, hi, qi, *_: (bi, nq - 1 - qi, hi)
    head_all = lambda bi, hi, qi, *_: (bi, 0, hi)
    tile = (1, tq, LANES)
    grid_spec = pltpu.PrefetchScalarGridSpec(
        num_scalar_prefetch=2,
        grid=(b, A_HEADS, nq),
        in_specs=[
            pl.BlockSpec(tile, fwd),
            pl.BlockSpec((1, s, LANES), head_all, pipeline_mode=once),
            pl.BlockSpec((1, s, V_AUG), head_all, pipeline_mode=once),
            pl.BlockSpec((1, n_d, tq, tk), lambda bi, hi, qi, *_: (hi, 0, 0, 0), pipeline_mode=once),
            pl.BlockSpec((1, V_DIM), lambda bi, hi, qi, *_: (0, 0)),
            pl.BlockSpec(tile, fwd), pl.BlockSpec(tile, fwd), pl.BlockSpec(tile, fwd),
            pl.BlockSpec(tile, bwd), pl.BlockSpec(tile, bwd), pl.BlockSpec(tile, bwd),
            pl.BlockSpec((2, LANES), lambda bi, hi, qi, *_: (0, hi)),
        ],
        out_specs=(pl.BlockSpec(tile, fwd), pl.BlockSpec(tile, fwd), pl.BlockSpec(tile, bwd)),
        scratch_shapes=[pltpu.VMEM((2 * tq, 1), F32), pltpu.VMEM((2 * tq, V_AUG), F32),
                        pltpu.VMEM((HGRN_DV, HGRN_DK), F32), pltpu.VMEM((HGRN_DV, HGRN_DK), F32)],
    )
    return pl.pallas_call(
        functools.partial(_mixer_kernel, nk=nk),
        out_shape=(jax.ShapeDtypeStruct((b, s, A_HEADS * V_DIM), BF16),
                   jax.ShapeDtypeStruct((b, s, SEG), F32),
                   jax.ShapeDtypeStruct((b, s, SEG), F32)),
        grid_spec=grid_spec,
        compiler_params=pltpu.CompilerParams(
            dimension_semantics=("arbitrary", "arbitrary", "arbitrary"),
            vmem_limit_bytes=VMEM_LIMIT),
        name="mixer",
    )(p["fast"], p["lam"], qa, ka, va, p["bias_tiles"], p["subln_w"],
      qh, ffw, ih, qh, fbw, ih, p["lb"])


def _hgrn_group(q_ref, f_ref, v_ref, o_ref, st, lb, row0, g_chunks, reverse):
    rows_g = g_chunks * CHUNK
    r = lax.broadcasted_iota(jnp.int32, (CHUNK, CHUNK), 0)
    c = lax.broadcasted_iota(jnp.int32, (CHUNK, CHUNK), 1)
    pos = lax.broadcasted_iota(jnp.int32, (rows_g, LANES), 0) % CHUNK
    nt = (((1,), (1,)), ((), ()))
    tn = (((0,), (0,)), ((), ()))
    rows = pl.ds(row0, rows_g)
    q = q_ref[0, rows, :].astype(F32)
    v = v_ref[0, rows, :]
    t = jnp.tanh(0.5 * f_ref[0, rows, :])
    half = 0.5 * (1.0 - lb)
    f = (lb + half) + half * t
    k = half * (1.0 - t)
    b = jnp.log(f)
    step = 1
    while step < CHUNK:
        if reverse:
            b = b + jnp.where(pos < CHUNK - step, pltpu.roll(b, rows_g - step, 0), 0.0)
        else:
            b = b + jnp.where(pos >= step, pltpu.roll(b, step, 0), 0.0)
        step *= 2
    last_row = 0 if reverse else CHUNK - 1
    mask = (r <= c) if reverse else (r >= c)
    b3 = b.reshape(g_chunks, CHUNK, LANES)
    b_last = b3[:, last_row:last_row + 1, :]
    decay = jnp.exp(b_last)
    q_t = (q * jnp.exp(b)).astype(BF16).reshape(g_chunks, CHUNK, LANES)
    k_t = (k * jnp.exp(-b)).astype(BF16).reshape(g_chunks, CHUNK, LANES)
    k_d = (k.reshape(g_chunks, CHUNK, LANES) * jnp.exp(b_last - b3)).astype(BF16)
    v3 = v.reshape(g_chunks, CHUNK, LANES)
    a = [jnp.where(mask, lax.dot_general(q_t[g], k_t[g], nt, preferred_element_type=F32),
                   0.0).astype(BF16) for g in range(g_chunks)]
    ds = [lax.dot_general(v3[g], k_d[g], tn, preferred_element_type=F32)
          for g in range(g_chunks)]
    o_intra = [jnp.dot(a[g], v3[g], preferred_element_type=F32) for g in range(g_chunks)]
    outs = [None] * g_chunks
    for g in (range(g_chunks - 1, -1, -1) if reverse else range(g_chunks)):
        outs[g] = o_intra[g] + lax.dot_general(q_t[g], st.astype(BF16), nt,
                                               preferred_element_type=F32)
        st = decay[g] * st + ds[g]
    o_ref[0, rows, :] = jnp.concatenate(outs, axis=0)
    return st


def _mlp_kernel(x_ref, oa_ref, of_ref, ob_ref, g_ref, hw_ref, wo_ref, mw_ref, w1_ref, w2_ref, y_ref):
    o = of_ref[...] + ob_ref[...]
    g = g_ref[...]
    parts = [oa_ref[...]]
    for hh in range(HGRN_HEADS):
        sl = slice(hh * HGRN_DV, (hh + 1) * HGRN_DV)
        oh = o[:, sl]
        ms = jnp.mean(oh * oh, axis=-1, keepdims=True)
        gh = g[:, sl]
        parts.append((oh * lax.rsqrt(ms + EPS) * hw_ref[...] * _silu(gh)).astype(BF16))
    cat = jnp.concatenate(parts, axis=1)
    x1 = x_ref[...] + jnp.dot(cat, wo_ref[...], preferred_element_type=F32)
    ms = jnp.mean(x1 * x1, axis=-1, keepdims=True)
    h2 = (x1 * lax.rsqrt(ms + EPS) * mw_ref[...]).astype(BF16)
    acc = x1
    for cc in range(D_FF // D_MODEL):
        sl = slice(cc * D_MODEL, (cc + 1) * D_MODEL)
        u = jnp.dot(h2, w1_ref[:, sl], preferred_element_type=F32)
        a = jnp.square(jnp.maximum(u, 0.0)).astype(BF16)
        acc = acc + jnp.dot(a, w2_ref[sl, :], preferred_element_type=F32)
    y_ref[...] = acc


def _mlp_call(x2d, oa, o_f, o_b, gh, p):
    t = x2d.shape[0]
    tm = TM_MLP
    assert t % tm == 0
    row = lambda i: (i, 0)
    const = lambda i: (0, 0)
    once = pl.Buffered(1)
    return pl.pallas_call(
        _mlp_kernel,
        out_shape=jax.ShapeDtypeStruct((t, D_MODEL), F32),
        grid=(t // tm,),
        in_specs=[
            pl.BlockSpec((tm, D_MODEL), row),
            pl.BlockSpec((tm, SEG), row),
            pl.BlockSpec((tm, SEG), row),
            pl.BlockSpec((tm, SEG), row),
            pl.BlockSpec((tm, SEG), row),
            pl.BlockSpec((1, HGRN_DV), const),
            pl.BlockSpec((D_MODEL, D_MODEL), const, pipeline_mode=once),
            pl.BlockSpec((1, D_MODEL), const),
            pl.BlockSpec((D_MODEL, D_FF), const, pipeline_mode=once),
            pl.BlockSpec((D_FF, D_MODEL), const, pipeline_mode=once),
        ],
        out_specs=pl.BlockSpec((tm, D_MODEL), row),
        compiler_params=pltpu.CompilerParams(
            dimension_semantics=("arbitrary",), vmem_limit_bytes=VMEM_LIMIT),
        name="out_mlp",
    )(x2d, oa, o_f, o_b, gh, p["hgrn_norm_w"], p["w_out"], p["mlp_norm_w"], p["w_mlp_in"], p["w_mlp_out"])


def _prepare(attn_norm_w, w_in, qk_norm_w, diff_lambda, diff_subln_w, rel_bias, hgrn_lb,
             hgrn_norm_w, w_out, mlp_norm_w, w_mlp_in, w_mlp_out):
    l = 0
    lam_p = diff_lambda[l].astype(F32)
    lam = jnp.exp(jnp.sum(lam_p[0] * lam_p[1])) - jnp.exp(jnp.sum(lam_p[2] * lam_p[3])) + LAM_INIT
    lb = jnp.cumsum(jax.nn.softmax(hgrn_lb.astype(F32), axis=1), axis=1)[:, l]

    rb = rel_bias.astype(F32)
    left = jnp.take(rb, _t5_bucket(jnp.array([BAND_OFFSETS[0] * TK])), axis=0)
    table = (rb - left) * LOG2E
    ii = jnp.arange(TQ)[:, None]
    jj = jnp.arange(TK)[None, :]
    buckets = _t5_bucket(jnp.stack([d * TK + jj - ii for d in BAND_OFFSETS]))
    tiles = _bias_call(table, buckets.astype(jnp.int32))

    wq_eff = qk_norm_w[l, 0].astype(F32) * (QK_DIM ** -0.5 * LOG2E)
    wk_eff = qk_norm_w[l, 1].astype(F32)
    logit_bound = 1.02 * QK_DIM * jnp.max(jnp.abs(wq_eff * wk_eff)) + 2.0 * jnp.max(jnp.abs(table))
    fast = (logit_bound <= FAST_EXP_RANGE).astype(jnp.int32).reshape(1)

    reps = SEG // QK_DIM
    grp = jnp.arange(SEG) // QK_DIM
    group_mat = jnp.where(grp[:, None] == grp[None, :], 1.0 / QK_DIM, 0.0).astype(BF16)
    return dict(
        attn_norm_w=attn_norm_w[l][None].astype(F32),
        w_in=w_in[l].astype(BF16),
        group_mat=group_mat,
        wq=jnp.tile(wq_eff, reps)[None],
        wk=jnp.tile(wk_eff, reps)[None],
        fast=fast,
        lam=jnp.reshape(lam, (1,)).astype(F32),
        bias_tiles=tiles,
        subln_w=diff_subln_w[l][None].astype(F32),
        lb=lb,
        hgrn_norm_w=hgrn_norm_w[l][None].astype(F32),
        w_out=w_out[l].astype(BF16),
        mlp_norm_w=mlp_norm_w[l][None].astype(F32),
        w_mlp_in=w_mlp_in[l].astype(BF16),
        w_mlp_out=w_mlp_out[l].astype(BF16),
    )


def _layer(x, p):
    b, s, d = x.shape
    x2d = x.reshape(b * s, d)
    qa, ka, va, qh, ffw, fbw, ih, gh = _proj_call(x2d, p)
    r3 = lambda t: t.reshape(b, s, t.shape[-1])
    oa, o_f, o_b = _mixer_call(r3(qa), r3(ka), r3(va), r3(qh), r3(ffw), r3(fbw), r3(ih), p)
    y = _mlp_call(x2d, oa.reshape(b * s, -1), o_f.reshape(b * s, -1), o_b.reshape(b * s, -1), gh, p)
    return y.reshape(b, s, d)


def kernel(x_prompt, x_sample, attn_norm_w, w_in, qk_norm_w, diff_lambda, diff_subln_w, rel_bias,
           hgrn_lb, hgrn_norm_w, w_out, mlp_norm_w, w_mlp_in, w_mlp_out):
    p = _prepare(attn_norm_w, w_in, qk_norm_w, diff_lambda, diff_subln_w, rel_bias, hgrn_lb,
                 hgrn_norm_w, w_out, mlp_norm_w, w_mlp_in, w_mlp_out)
    return _layer(x_prompt, p), _layer(x_sample, p)
```

```python
import functools
import math

import jax
import jax.numpy as jnp
from jax import lax
from jax.experimental import pallas as pl
from jax.experimental.pallas import tpu as pltpu

F32 = jnp.float32
BF16 = jnp.bfloat16

D_MODEL = 1024
A_HEADS = 4
V_DIM = 128
QK_DIM = 64
HGRN_HEADS = 4
HGRN_DK = 128
HGRN_DV = 128
SEG = 512
N_SEG = 8
D_FF = 4 * D_MODEL
N_BUCKETS = 32
MAX_DISTANCE = 128
CHUNK = 64
EPS = 1e-6
LOG2E = math.log2(math.e)
LAM_INIT = 0.8 - 0.6 * math.exp(-0.3 * 0)

LANES = 128
V_AUG = 2 * LANES
GROUP_MAT = 2 * LANES
VMEM_LIMIT = 56 * 1024 * 1024

TM_PROJ = 512
TM_MLP = 512
TQ = 1024
TK = 512
KEY_GROUP = 8
BAND_OFFSETS = tuple(range(-2, TQ // TK + 2))
NEG_BIG = -1e30
FAST_EXP_RANGE = 60.0


def _silu(x):
    return (0.5 * x) * (1.0 + jnp.tanh(0.5 * x))


def _t5_bucket(rel):
    nb = N_BUCKETS // 2
    max_exact = nb // 2
    ret = jnp.where(rel > 0, nb, 0)
    n = jnp.abs(rel)
    nf = jnp.maximum(n, 1).astype(jnp.float32)
    large = max_exact + (jnp.log(nf / max_exact) / math.log(MAX_DISTANCE / max_exact)
                         * (nb - max_exact)).astype(jnp.int32)
    large = jnp.minimum(large, nb - 1)
    return ret + jnp.where(n < max_exact, n, large)


def _proj_kernel(x_ref, nw_ref, w_ref, g_ref, wq_ref, wk_ref,
                 qa_ref, ka_ref, va_ref, qh_ref, ffw_ref, fbw_ref, ih_ref, gh_ref):
    x = x_ref[...]
    ms = jnp.mean(x * x, axis=-1, keepdims=True)
    h = (x * lax.rsqrt(ms + EPS) * nw_ref[...]).astype(BF16)

    def seg(i):
        return jnp.dot(h, w_ref[:, i * SEG:(i + 1) * SEG], preferred_element_type=F32)

    def group_norm(t, w):
        sq = (t * t).astype(BF16)
        half = g_ref.shape[0]
        ss = jnp.concatenate(
            [jnp.dot(sq[:, c0:c0 + half], g_ref[...], preferred_element_type=F32)
             for c0 in range(0, SEG, half)], axis=1)
        return (t * lax.rsqrt(ss + EPS) * w).astype(BF16)

    qa_ref[...] = group_norm(seg(0), wq_ref[...])
    ka_ref[...] = group_norm(seg(1), wk_ref[...])

    v = seg(2).astype(BF16)
    tm = v.shape[0]
    lane = lax.broadcasted_iota(jnp.int32, (tm, LANES), 1)
    ones_col = jnp.where(lane == 0, 1.0, 0.0).astype(BF16)
    for hh in range(A_HEADS):
        va_ref[:, hh * V_AUG:hh * V_AUG + V_DIM] = v[:, hh * V_DIM:(hh + 1) * V_DIM]
        va_ref[:, hh * V_AUG + V_DIM:(hh + 1) * V_AUG] = ones_col

    qh = seg(3)
    qh_ref[...] = (_silu(qh) * (HGRN_DK ** -0.5)).astype(BF16)
    ffw_ref[...] = seg(4)
    fbw_ref[...] = seg(5)
    ih_ref[...] = seg(6).astype(BF16)
    gh_ref[...] = seg(7)


def _proj_call(x2d, p):
    t = x2d.shape[0]
    tm = TM_PROJ
    assert t % tm == 0
    row = lambda i: (i, 0)
    const = lambda i: (0, 0)
    out_shape = (
        jax.ShapeDtypeStruct((t, SEG), BF16),
        jax.ShapeDtypeStruct((t, SEG), BF16),
        jax.ShapeDtypeStruct((t, A_HEADS * V_AUG), BF16),
        jax.ShapeDtypeStruct((t, SEG), BF16),
        jax.ShapeDtypeStruct((t, SEG), F32),
        jax.ShapeDtypeStruct((t, SEG), F32),
        jax.ShapeDtypeStruct((t, SEG), BF16),
        jax.ShapeDtypeStruct((t, SEG), F32),
    )
    out_specs = tuple(pl.BlockSpec((tm, s.shape[1]), row) for s in out_shape)
    return pl.pallas_call(
        _proj_kernel,
        out_shape=out_shape,
        grid=(t // tm,),
        in_specs=[
            pl.BlockSpec((tm, D_MODEL), row),
            pl.BlockSpec((1, D_MODEL), const),
            pl.BlockSpec((D_MODEL, N_SEG * SEG), const),
            pl.BlockSpec((GROUP_MAT, GROUP_MAT), const),
            pl.BlockSpec((1, SEG), const),
            pl.BlockSpec((1, SEG), const),
        ],
        out_specs=out_specs,
        compiler_params=pltpu.CompilerParams(
            dimension_semantics=("arbitrary",), vmem_limit_bytes=VMEM_LIMIT),
        name="proj",
    )(x2d, p["attn_norm_w"], p["w_in"], p["group_mat"], p["wq"], p["wk"])


def _bias_kernel(tab_ref, bucket_ref, o_ref):
    hh = pl.program_id(0)
    _, n_d, tq, tk = o_ref.shape
    width = bucket_ref.shape[1]
    rows = 64
    bucket = bucket_ref[...]
    line = jnp.zeros(bucket.shape, F32)
    for bb in range(N_BUCKETS):
        line = jnp.where(bucket == bb, tab_ref[bb, hh], line)
    spread = jnp.broadcast_to(line, (rows, width))

    def body(i, carry):
        r0 = pl.multiple_of(i * rows, rows)
        rot = pltpu.roll(spread, r0, 1, stride=1, stride_axis=0)
        for d_idx, d in enumerate(BAND_OFFSETS):
            c0 = (d * tk) % width
            o_ref[0, d_idx, pl.ds(r0, rows), :] = rot[:, c0:c0 + tk]
        return carry

    lax.fori_loop(0, tq // rows, body, 0)


def _bias_call(table, line_buckets):
    n_d = len(BAND_OFFSETS)
    width = line_buckets.shape[1]
    return pl.pallas_call(
        _bias_kernel,
        out_shape=jax.ShapeDtypeStruct((A_HEADS, n_d, TQ, TK), F32),
        grid=(A_HEADS,),
        in_specs=[
            pl.BlockSpec(memory_space=pltpu.SMEM),
            pl.BlockSpec((1, width), lambda hi: (0, 0)),
        ],
        out_specs=pl.BlockSpec((1, n_d, TQ, TK), lambda hi: (hi, 0, 0, 0)),
        compiler_params=pltpu.CompilerParams(
            dimension_semantics=("arbitrary",), vmem_limit_bytes=VMEM_LIMIT),
        name="bias_tiles",
    )(table, line_buckets)


def _mixer_kernel(fast_ref, lam_ref, q_ref, k_ref, v_ref, bias_ref, sw_ref,
                  hqf_ref, hff_ref, hvf_ref, hqb_ref, hfb_ref, hvb_ref, lb_ref,
                  o_ref, of_ref, ob_ref, m_ref, acc_ref, sf_ref, sb_ref, *, nk):
    qi = pl.program_id(2)
    tq = q_ref.shape[1]
    tk = bias_ref.shape[3]
    n_trips = nk // KEY_GROUP
    g_chunks = tq // (CHUNK * n_trips)
    rows_g = g_chunks * CHUNK

    @pl.when(qi == 0)
    def _():
        sf_ref[...] = jnp.zeros(sf_ref.shape, F32)
        sb_ref[...] = jnp.zeros(sb_ref.shape, F32)

    lb_f = lb_ref[0:1, :]
    lb_b = lb_ref[1:2, :]

    def hgrn_step(g):
        row_f = pl.multiple_of(g * rows_g, rows_g)
        sf_ref[...] = _hgrn_group(hqf_ref, hff_ref, hvf_ref, of_ref, sf_ref[...], lb_f, row_f,
                                  g_chunks, False)
        row_b = pl.multiple_of((n_trips - 1 - g) * rows_g, rows_g)
        sb_ref[...] = _hgrn_group(hqb_ref, hfb_ref, hvb_ref, ob_ref, sb_ref[...], lb_b, row_b,
                                  g_chunks, True)

    q = q_ref[0]
    lane = lax.broadcasted_iota(jnp.int32, q.shape, 1)
    zero = jnp.zeros_like(q)
    qs = jnp.concatenate([jnp.where(lane < QK_DIM, q, zero),
                          jnp.where(lane >= QK_DIM, q, zero)], axis=0)

    def scores(j):
        k = k_ref[0, pl.ds(pl.multiple_of(j * tk, tk), tk), :]
        return lax.dot_general(qs, k, (((1,), (1,)), ((), ())), preferred_element_type=F32)

    def values(j):
        return v_ref[0, pl.ds(pl.multiple_of(j * tk, tk), tk), :]

    def add_bias(s, bias):
        return (s.reshape(2, tq, tk) + bias[None]).reshape(2 * tq, tk)

    def for_tiles(lo, hi, fn):
        def body(j, carry):
            fn(j)
            return carry
        lax.fori_loop(lo, hi, body, 0)

    def biased_scores(j):
        d_idx = jnp.clip(j - qi * (tq // tk), BAND_OFFSETS[0], BAND_OFFSETS[-1]) - BAND_OFFSETS[0]
        return add_bias(scores(j), bias_ref[0, d_idx])

    acc_ref[...] = jnp.zeros(acc_ref.shape, F32)

    @pl.when(fast_ref[0] == 1)
    def _():
        def group(g):
            for u in range(KEY_GROUP):
                j = g * KEY_GROUP + u
                p = jnp.exp2(biased_scores(j)).astype(BF16)
                acc_ref[...] += jnp.dot(p, values(j), preferred_element_type=F32)
            hgrn_step(g)

        for_tiles(0, n_trips, group)

    @pl.when(fast_ref[0] != 1)
    def _():
        m_ref[...] = jnp.full(m_ref.shape, NEG_BIG, F32)

        def update(j):
            s = biased_scores(j)
            m_prev = m_ref[...]
            m_new = jnp.maximum(m_prev, jnp.max(s, axis=-1, keepdims=True))
            alpha = jnp.exp2(m_prev - m_new)
            p = jnp.exp2(s - m_new).astype(BF16)
            acc_ref[...] = alpha * acc_ref[...] + jnp.dot(p, values(j), preferred_element_type=F32)
            m_ref[...] = m_new

        for_tiles(0, n_trips, hgrn_step)
        for_tiles(0, nk, update)

    acc = acc_ref[...]
    o1 = acc[:tq, :V_DIM] / acc[:tq, V_DIM:V_DIM + 1]
    o2 = acc[tq:, :V_DIM] / acc[tq:, V_DIM:V_DIM + 1]
    o = o1 - lam_ref[0] * o2
    ms = jnp.mean(o * o, axis=-1, keepdims=True)
    o = o * lax.rsqrt(ms + EPS) * sw_ref[...] * (1.0 - LAM_INIT)
    o_ref[0] = o.astype(BF16)


def _mixer_call(qa, ka, va, qh, ffw, fbw, ih, p):
    b, s, _ = qa.shape
    tq, tk = TQ, TK
    assert tq % tk == 0 and s % tq == 0 and A_HEADS == HGRN_HEADS
    nq, nk = s // tq, s // tk
    assert nk % KEY_GROUP == 0 and tq % (CHUNK * (nk // KEY_GROUP)) == 0
    n_d = len(BAND_OFFSETS)
    fwd = lambda hi, bi, qi, *_: (bi, qi, hi)
    bwd = lambda hi, bi, qi, *_: (bi, nq - 1 - qi, hi)
    head_all = lambda hi, bi, qi, *_: (bi, 0, hi)
    tile = (1, tq, LANES)
    grid_spec = pltpu.PrefetchScalarGridSpec(
        num_scalar_prefetch=2,
        grid=(A_HEADS, b, nq),
        in_specs=[
            pl.BlockSpec(tile, fwd),
            pl.BlockSpec((1, s, LANES), head_all),
            pl.BlockSpec((1, s, V_AUG), head_all),
            pl.BlockSpec((1, n_d, tq, tk), lambda hi, bi, qi, *_: (hi, 0, 0, 0),
                         pipeline_mode=pl.Buffered(1)),
            pl.BlockSpec((1, V_DIM), lambda hi, bi, qi, *_: (0, 0)),
            pl.BlockSpec(tile, fwd), pl.BlockSpec(tile, fwd), pl.BlockSpec(tile, fwd),
            pl.BlockSpec(tile, bwd), pl.BlockSpec(tile, bwd), pl.BlockSpec(tile, bwd),
            pl.BlockSpec((2, LANES), lambda hi, bi, qi, *_: (0, hi)),
        ],
        out_specs=(pl.BlockSpec(tile, fwd), pl.BlockSpec(tile, fwd), pl.BlockSpec(tile, bwd)),
        scratch_shapes=[pltpu.VMEM((2 * tq, 1), F32), pltpu.VMEM((2 * tq, V_AUG), F32),
                        pltpu.VMEM((HGRN_DV, HGRN_DK), F32), pltpu.VMEM((HGRN_DV, HGRN_DK), F32)],
    )
    return pl.pallas_call(
        functools.partial(_mixer_kernel, nk=nk),
        out_shape=(jax.ShapeDtypeStruct((b, s, A_HEADS * V_DIM), BF16),
                   jax.ShapeDtypeStruct((b, s, SEG), F32),
                   jax.ShapeDtypeStruct((b, s, SEG), F32)),
        grid_spec=grid_spec,
        compiler_params=pltpu.CompilerParams(
            dimension_semantics=("arbitrary", "arbitrary", "arbitrary"),
            vmem_limit_bytes=VMEM_LIMIT),
        name="mixer",
    )(p["fast"], p["lam"], qa, ka, va, p["bias_tiles"], p["subln_w"],
      qh, ffw, ih, qh, fbw, ih, p["lb"])


def _hgrn_group(q_ref, f_ref, v_ref, o_ref, st, lb, row0, g_chunks, reverse):
    rows_g = g_chunks * CHUNK
    r = lax.broadcasted_iota(jnp.int32, (CHUNK, CHUNK), 0)
    c = lax.broadcasted_iota(jnp.int32, (CHUNK, CHUNK), 1)
    pos = lax.broadcasted_iota(jnp.int32, (rows_g, LANES), 0) % CHUNK
    nt = (((1,), (1,)), ((), ()))
    tn = (((0,), (0,)), ((), ()))
    rows = pl.ds(row0, rows_g)
    q = q_ref[0, rows, :].astype(F32)
    v = v_ref[0, rows, :]
    t = jnp.tanh(0.5 * f_ref[0, rows, :])
    half = 0.5 * (1.0 - lb)
    f = (lb + half) + half * t
    k = half * (1.0 - t)
    b = jnp.log(f)
    step = 1
    while step < CHUNK:
        if reverse:
            b = b + jnp.where(pos < CHUNK - step, pltpu.roll(b, rows_g - step, 0), 0.0)
        else:
            b = b + jnp.where(pos >= step, pltpu.roll(b, step, 0), 0.0)
        step *= 2
    last_row = 0 if reverse else CHUNK - 1
    mask = (r <= c) if reverse else (r >= c)
    b3 = b.reshape(g_chunks, CHUNK, LANES)
    b_last = b3[:, last_row:last_row + 1, :]
    decay = jnp.exp(b_last)
    q_t = (q * jnp.exp(b)).astype(BF16).reshape(g_chunks, CHUNK, LANES)
    k_t = (k * jnp.exp(-b)).astype(BF16).reshape(g_chunks, CHUNK, LANES)
    k_d = (k.reshape(g_chunks, CHUNK, LANES) * jnp.exp(b_last - b3)).astype(BF16)
    v3 = v.reshape(g_chunks, CHUNK, LANES)
    a = [jnp.where(mask, lax.dot_general(q_t[g], k_t[g], nt, preferred_element_type=F32),
                   0.0).astype(BF16) for g in range(g_chunks)]
    ds = [lax.dot_general(v3[g], k_d[g], tn, preferred_element_type=F32)
          for g in range(g_chunks)]
    o_intra = [jnp.dot(a[g], v3[g], preferred_element_type=F32) for g in range(g_chunks)]
    outs = [None] * g_chunks
    for g in (range(g_chunks - 1, -1, -1) if reverse else range(g_chunks)):
        outs[g] = o_intra[g] + lax.dot_general(q_t[g], st.astype(BF16), nt,
                                               preferred_element_type=F32)
        st = decay[g] * st + ds[g]
    o_ref[0, rows, :] = jnp.concatenate(outs, axis=0)
    return st


def _mlp_kernel(x_ref, oa_ref, of_ref, ob_ref, g_ref, hw_ref, wo_ref, mw_ref, w1_ref, w2_ref, y_ref):
    o = of_ref[...] + ob_ref[...]
    g = g_ref[...]
    parts = [oa_ref[...]]
    for hh in range(HGRN_HEADS):
        sl = slice(hh * HGRN_DV, (hh + 1) * HGRN_DV)
        oh = o[:, sl]
        ms = jnp.mean(oh * oh, axis=-1, keepdims=True)
        gh = g[:, sl]
        parts.append((oh * lax.rsqrt(ms + EPS) * hw_ref[...] * _silu(gh)).astype(BF16))
    cat = jnp.concatenate(parts, axis=1)
    x1 = x_ref[...] + jnp.dot(cat, wo_ref[...], preferred_element_type=F32)
    ms = jnp.mean(x1 * x1, axis=-1, keepdims=True)
    h2 = (x1 * lax.rsqrt(ms + EPS) * mw_ref[...]).astype(BF16)
    acc = x1
    for cc in range(D_FF // D_MODEL):
        sl = slice(cc * D_MODEL, (cc + 1) * D_MODEL)
        u = jnp.dot(h2, w1_ref[:, sl], preferred_element_type=F32)
        a = jnp.square(jnp.maximum(u, 0.0)).astype(BF16)
        acc = acc + jnp.dot(a, w2_ref[sl, :], preferred_element_type=F32)
    y_ref[...] = acc


def _mlp_call(x2d, oa, o_f, o_b, gh, p):
    t = x2d.shape[0]
    tm = TM_MLP
    assert t % tm == 0
    row = lambda i: (i, 0)
    const = lambda i: (0, 0)
    once = pl.Buffered(1)
    return pl.pallas_call(
        _mlp_kernel,
        out_shape=jax.ShapeDtypeStruct((t, D_MODEL), F32),
        grid=(t // tm,),
        in_specs=[
            pl.BlockSpec((tm, D_MODEL), row),
            pl.BlockSpec((tm, SEG), row),
            pl.BlockSpec((tm, SEG), row),
            pl.BlockSpec((tm, SEG), row),
            pl.BlockSpec((tm, SEG), row),
            pl.BlockSpec((1, HGRN_DV), const),
            pl.BlockSpec((D_MODEL, D_MODEL), const, pipeline_mode=once),
            pl.BlockSpec((1, D_MODEL), const),
            pl.BlockSpec((D_MODEL, D_FF), const, pipeline_mode=once),
            pl.BlockSpec((D_FF, D_MODEL), const, pipeline_mode=once),
        ],
        out_specs=pl.BlockSpec((tm, D_MODEL), row),
        compiler_params=pltpu.CompilerParams(
            dimension_semantics=("arbitrary",), vmem_limit_bytes=VMEM_LIMIT),
        name="out_mlp",
    )(x2d, oa, o_f, o_b, gh, p["hgrn_norm_w"], p["w_out"], p["mlp_norm_w"], p["w_mlp_in"], p["w_mlp_out"])


def _prepare(attn_norm_w, w_in, qk_norm_w, diff_lambda, diff_subln_w, rel_bias, hgrn_lb,
             hgrn_norm_w, w_out, mlp_norm_w, w_mlp_in, w_mlp_out):
    l = 0
    lam_p = diff_lambda[l].astype(F32)
    lam = jnp.exp(jnp.sum(lam_p[0] * lam_p[1])) - jnp.exp(jnp.sum(lam_p[2] * lam_p[3])) + LAM_INIT
    lb = jnp.cumsum(jax.nn.softmax(hgrn_lb.astype(F32), axis=1), axis=1)[:, l]

    rb = rel_bias.astype(F32)
    left = jnp.take(rb, _t5_bucket(jnp.array([BAND_OFFSETS[0] * TK])), axis=0)
    table = (rb - left) * LOG2E
    reach = max(-BAND_OFFSETS[0] * TK + TQ - 1, BAND_OFFSETS[-1] * TK + TK - 1)
    width = 2 * pl.next_power_of_2(reach + 1)
    u = jnp.arange(width)
    line_buckets = _t5_bucket(jnp.where(u < width // 2, u, u - width))[None].astype(jnp.int32)
    tiles = _bias_call(table, line_buckets)

    wq_eff = qk_norm_w[l, 0].astype(F32) * (QK_DIM ** -0.5 * LOG2E)
    wk_eff = qk_norm_w[l, 1].astype(F32)
    logit_bound = 1.02 * QK_DIM * jnp.max(jnp.abs(wq_eff * wk_eff)) + 2.0 * jnp.max(jnp.abs(table))
    fast = (logit_bound <= FAST_EXP_RANGE).astype(jnp.int32).reshape(1)

    reps = SEG // QK_DIM
    grp = jnp.arange(GROUP_MAT) // QK_DIM
    group_mat = jnp.where(grp[:, None] == grp[None, :], 1.0 / QK_DIM, 0.0).astype(BF16)
    return dict(
        attn_norm_w=attn_norm_w[l][None].astype(F32),
        w_in=w_in[l].astype(BF16),
        group_mat=group_mat,
        wq=jnp.tile(wq_eff, reps)[None],
        wk=jnp.tile(wk_eff, reps)[None],
        fast=fast,
        lam=jnp.reshape(lam, (1,)).astype(F32),
        bias_tiles=tiles,
        subln_w=diff_subln_w[l][None].astype(F32),
        lb=lb,
        hgrn_norm_w=hgrn_norm_w[l][None].astype(F32),
        w_out=w_out[l].astype(BF16),
        mlp_norm_w=mlp_norm_w[l][None].astype(F32),
        w_mlp_in=w_mlp_in[l].astype(BF16),
        w_mlp_out=w_mlp_out[l].astype(BF16),
    )


def _layer(x, p):
    b, s, d = x.shape
    x2d = x.reshape(b * s, d)
    qa, ka, va, qh, ffw, fbw, ih, gh = _proj_call(x2d, p)
    r3 = lambda t: t.reshape(b, s, t.shape[-1])
    oa, o_f, o_b = _mixer_call(r3(qa), r3(ka), r3(va), r3(qh), r3(ffw), r3(fbw), r3(ih), p)
    y = _mlp_call(x2d, oa.reshape(b * s, -1), o_f.reshape(b * s, -1), o_b.reshape(b * s, -1), gh, p)
    return y.reshape(b, s, d)


def kernel(x_prompt, x_sample, attn_norm_w, w_in, qk_norm_w, diff_lambda, diff_subln_w, rel_bias,
           hgrn_lb, hgrn_norm_w, w_out, mlp_norm_w, w_mlp_in, w_mlp_out):
    p = _prepare(attn_norm_w, w_in, qk_norm_w, diff_lambda, diff_subln_w, rel_bias, hgrn_lb,
                 hgrn_norm_w, w_out, mlp_norm_w, w_mlp_in, w_mlp_out)
    return _layer(x_prompt, p), _layer(x_sample, p)
```

```python
import functools
import math

import jax
import jax.numpy as jnp
from jax import lax
from jax.experimental import pallas as pl
from jax.experimental.pallas import tpu as pltpu

F32 = jnp.float32
BF16 = jnp.bfloat16

D_MODEL = 1024
A_HEADS = 4
V_DIM = 128
QK_DIM = 64
HGRN_HEADS = 4
HGRN_DK = 128
HGRN_DV = 128
SEG = 512
N_SEG = 8
D_FF = 4 * D_MODEL
N_BUCKETS = 32
MAX_DISTANCE = 128
CHUNK = 64
EPS = 1e-6
LOG2E = math.log2(math.e)
LAM_INIT = 0.8 - 0.6 * math.exp(-0.3 * 0)

LANES = 128
V_AUG = 2 * LANES
GROUP_MAT = 2 * LANES
VMEM_LIMIT = 56 * 1024 * 1024

TM_PROJ = 512
TM_MLP = 512
TQ = 1024
TK = 512
KEY_GROUP = 8
BAND_OFFSETS = tuple(range(-2, TQ // TK + 2))
NEG_BIG = -1e30
FAST_EXP_RANGE = 60.0


def _silu(x):
    return (0.5 * x) * (1.0 + jnp.tanh(0.5 * x))


def _t5_bucket(rel):
    nb = N_BUCKETS // 2
    max_exact = nb // 2
    ret = jnp.where(rel > 0, nb, 0)
    n = jnp.abs(rel)
    nf = jnp.maximum(n, 1).astype(jnp.float32)
    large = max_exact + (jnp.log(nf / max_exact) / math.log(MAX_DISTANCE / max_exact)
                         * (nb - max_exact)).astype(jnp.int32)
    large = jnp.minimum(large, nb - 1)
    return ret + jnp.where(n < max_exact, n, large)


def _proj_kernel(x_ref, nw_ref, w_ref, g_ref, wq_ref, wk_ref,
                 qa_ref, ka_ref, va_ref, qh_ref, ffw_ref, fbw_ref, ih_ref, gh_ref):
    x = x_ref[...]
    ms = jnp.mean(x * x, axis=-1, keepdims=True)
    h = (x * lax.rsqrt(ms + EPS) * nw_ref[...]).astype(BF16)

    def seg(i):
        return jnp.dot(h, w_ref[:, i * SEG:(i + 1) * SEG], preferred_element_type=F32)

    def group_norm(t, w):
        sq = (t * t).astype(BF16)
        half = g_ref.shape[0]
        ss = jnp.concatenate(
            [jnp.dot(sq[:, c0:c0 + half], g_ref[...], preferred_element_type=F32)
             for c0 in range(0, SEG, half)], axis=1)
        return (t * lax.rsqrt(ss + EPS) * w).astype(BF16)

    qa_ref[...] = group_norm(seg(0), wq_ref[...])
    ka_ref[...] = group_norm(seg(1), wk_ref[...])

    v = seg(2).astype(BF16)
    ones = jnp.ones((v.shape[0], V_AUG - V_DIM), BF16)
    for hh in range(A_HEADS):
        va_ref[:, hh * V_AUG:hh * V_AUG + V_DIM] = v[:, hh * V_DIM:(hh + 1) * V_DIM]
        va_ref[:, hh * V_AUG + V_DIM:(hh + 1) * V_AUG] = ones

    qh = seg(3)
    qh_ref[...] = (_silu(qh) * (HGRN_DK ** -0.5)).astype(BF16)
    ffw_ref[...] = seg(4)
    fbw_ref[...] = seg(5)
    ih_ref[...] = seg(6).astype(BF16)
    gh_ref[...] = seg(7)


def _proj_call(x2d, p):
    t = x2d.shape[0]
    tm = TM_PROJ
    assert t % tm == 0
    row = lambda i: (i, 0)
    const = lambda i: (0, 0)
    out_shape = (
        jax.ShapeDtypeStruct((t, SEG), BF16),
        jax.ShapeDtypeStruct((t, SEG), BF16),
        jax.ShapeDtypeStruct((t, A_HEADS * V_AUG), BF16),
        jax.ShapeDtypeStruct((t, SEG), BF16),
        jax.ShapeDtypeStruct((t, SEG), F32),
        jax.ShapeDtypeStruct((t, SEG), F32),
        jax.ShapeDtypeStruct((t, SEG), BF16),
        jax.ShapeDtypeStruct((t, SEG), F32),
    )
    out_specs = tuple(pl.BlockSpec((tm, s.shape[1]), row) for s in out_shape)
    return pl.pallas_call(
        _proj_kernel,
        out_shape=out_shape,
        grid=(t // tm,),
        in_specs=[
            pl.BlockSpec((tm, D_MODEL), row),
            pl.BlockSpec((1, D_MODEL), const),
            pl.BlockSpec((D_MODEL, N_SEG * SEG), const),
            pl.BlockSpec((GROUP_MAT, GROUP_MAT), const),
            pl.BlockSpec((1, SEG), const),
            pl.BlockSpec((1, SEG), const),
        ],
        out_specs=out_specs,
        compiler_params=pltpu.CompilerParams(
            dimension_semantics=("arbitrary",), vmem_limit_bytes=VMEM_LIMIT),
        name="proj",
    )(x2d, p["attn_norm_w"], p["w_in"], p["group_mat"], p["wq"], p["wk"])


def _bias_kernel(tab_ref, bucket_ref, o_ref):
    hh = pl.program_id(0)
    _, n_d, tq, tk = o_ref.shape
    width = bucket_ref.shape[1]
    rows = 64
    bucket = bucket_ref[...]
    line = jnp.zeros(bucket.shape, F32)
    for bb in range(N_BUCKETS):
        line = jnp.where(bucket == bb, tab_ref[bb, hh], line)
    spread = jnp.broadcast_to(line, (rows, width))

    def body(i, carry):
        r0 = pl.multiple_of(i * rows, rows)
        rot = pltpu.roll(spread, r0, 1, stride=1, stride_axis=0)
        for d_idx, d in enumerate(BAND_OFFSETS):
            c0 = (d * tk) % width
            o_ref[0, d_idx, pl.ds(r0, rows), :] = rot[:, c0:c0 + tk]
        return carry

    lax.fori_loop(0, tq // rows, body, 0)


def _bias_call(table, line_buckets):
    n_d = len(BAND_OFFSETS)
    width = line_buckets.shape[1]
    return pl.pallas_call(
        _bias_kernel,
        out_shape=jax.ShapeDtypeStruct((A_HEADS, n_d, TQ, TK), F32),
        grid=(A_HEADS,),
        in_specs=[
            pl.BlockSpec(memory_space=pltpu.SMEM),
            pl.BlockSpec((1, width), lambda hi: (0, 0)),
        ],
        out_specs=pl.BlockSpec((1, n_d, TQ, TK), lambda hi: (hi, 0, 0, 0)),
        compiler_params=pltpu.CompilerParams(
            dimension_semantics=("arbitrary",), vmem_limit_bytes=VMEM_LIMIT),
        name="bias_tiles",
    )(table, line_buckets)


def _mixer_kernel(fast_ref, lam_ref, q_ref, k_ref, v_ref, bias_ref, sw_ref,
                  hqf_ref, hff_ref, hvf_ref, hqb_ref, hfb_ref, hvb_ref, lb_ref,
                  o_ref, of_ref, ob_ref, m_ref, acc_ref, sf_ref, sb_ref, *, nk):
    qi = pl.program_id(2)
    tq = q_ref.shape[1]
    tk = bias_ref.shape[3]
    n_trips = nk // KEY_GROUP
    g_chunks = tq // (CHUNK * n_trips)
    rows_g = g_chunks * CHUNK

    @pl.when(qi == 0)
    def _():
        sf_ref[...] = jnp.zeros(sf_ref.shape, F32)
        sb_ref[...] = jnp.zeros(sb_ref.shape, F32)

    lb_f = lb_ref[0:1, :]
    lb_b = lb_ref[1:2, :]

    def hgrn_step(g):
        row_f = pl.multiple_of(g * rows_g, rows_g)
        sf_ref[...] = _hgrn_group(hqf_ref, hff_ref, hvf_ref, of_ref, sf_ref[...], lb_f, row_f,
                                  g_chunks, False)
        row_b = pl.multiple_of((n_trips - 1 - g) * rows_g, rows_g)
        sb_ref[...] = _hgrn_group(hqb_ref, hfb_ref, hvb_ref, ob_ref, sb_ref[...], lb_b, row_b,
                                  g_chunks, True)

    q = q_ref[0]
    lane = lax.broadcasted_iota(jnp.int32, q.shape, 1)
    zero = jnp.zeros_like(q)
    qs = jnp.concatenate([jnp.where(lane < QK_DIM, q, zero),
                          jnp.where(lane >= QK_DIM, q, zero)], axis=0)

    def scores(j):
        k = k_ref[0, pl.ds(pl.multiple_of(j * tk, tk), tk), :]
        return lax.dot_general(qs, k, (((1,), (1,)), ((), ())), preferred_element_type=F32)

    def values(j):
        return v_ref[0, pl.ds(pl.multiple_of(j * tk, tk), tk), :]

    def add_bias(s, bias):
        return (s.reshape(2, tq, tk) + bias[None]).reshape(2 * tq, tk)

    def for_tiles(lo, hi, fn):
        def body(j, carry):
            fn(j)
            return carry
        lax.fori_loop(lo, hi, body, 0)

    def biased_scores(j):
        d_idx = jnp.clip(j - qi * (tq // tk), BAND_OFFSETS[0], BAND_OFFSETS[-1]) - BAND_OFFSETS[0]
        return add_bias(scores(j), bias_ref[0, d_idx])

    @pl.when(fast_ref[0] == 1)
    def _():
        def group(g, first):
            for u in range(KEY_GROUP):
                j = g * KEY_GROUP + u
                p = jnp.exp2(biased_scores(j)).astype(BF16)
                pv = jnp.dot(p, values(j), preferred_element_type=F32)
                if first and u == 0:
                    acc_ref[...] = pv
                else:
                    acc_ref[...] += pv
            hgrn_step(g)

        group(0, True)
        for_tiles(1, n_trips, lambda g: group(g, False))

    @pl.when(fast_ref[0] != 1)
    def _():
        m_ref[...] = jnp.full(m_ref.shape, NEG_BIG, F32)
        acc_ref[...] = jnp.zeros(acc_ref.shape, F32)

        def update(j):
            s = biased_scores(j)
            m_prev = m_ref[...]
            m_new = jnp.maximum(m_prev, jnp.max(s, axis=-1, keepdims=True))
            alpha = jnp.exp2(m_prev - m_new)
            p = jnp.exp2(s - m_new).astype(BF16)
            acc_ref[...] = alpha * acc_ref[...] + jnp.dot(p, values(j), preferred_element_type=F32)
            m_ref[...] = m_new

        for_tiles(0, n_trips, hgrn_step)
        for_tiles(0, nk, update)

    acc = acc_ref[...]
    o1 = acc[:tq, :V_DIM] / acc[:tq, V_DIM:]
    o2 = acc[tq:, :V_DIM] / acc[tq:, V_DIM:]
    o = o1 - lam_ref[0] * o2
    ms = jnp.mean(o * o, axis=-1, keepdims=True)
    o = o * lax.rsqrt(ms + EPS) * sw_ref[...] * (1.0 - LAM_INIT)
    o_ref[0] = o.astype(BF16)


def _mixer_call(qa, ka, va, qh, ffw, fbw, ih, p):
    b, s, _ = qa.shape
    tq, tk = TQ, TK
    assert tq % tk == 0 and s % tq == 0 and A_HEADS == HGRN_HEADS
    nq, nk = s // tq, s // tk
    assert nk % KEY_GROUP == 0 and tq % (CHUNK * (nk // KEY_GROUP)) == 0
    n_d = len(BAND_OFFSETS)
    fwd = lambda hi, bi, qi, *_: (bi, qi, hi)
    bwd = lambda hi, bi, qi, *_: (bi, nq - 1 - qi, hi)
    head_all = lambda hi, bi, qi, *_: (bi, 0, hi)
    tile = (1, tq, LANES)
    grid_spec = pltpu.PrefetchScalarGridSpec(
        num_scalar_prefetch=2,
        grid=(A_HEADS, b, nq),
        in_specs=[
            pl.BlockSpec(tile, fwd),
            pl.BlockSpec((1, s, LANES), head_all),
            pl.BlockSpec((1, s, V_AUG), head_all),
            pl.BlockSpec((1, n_d, tq, tk), lambda hi, bi, qi, *_: (hi, 0, 0, 0),
                         pipeline_mode=pl.Buffered(1)),
            pl.BlockSpec((1, V_DIM), lambda hi, bi, qi, *_: (0, 0)),
            pl.BlockSpec(tile, fwd), pl.BlockSpec(tile, fwd), pl.BlockSpec(tile, fwd),
            pl.BlockSpec(tile, bwd), pl.BlockSpec(tile, bwd), pl.BlockSpec(tile, bwd),
            pl.BlockSpec((2, LANES), lambda hi, bi, qi, *_: (0, hi)),
        ],
        out_specs=(pl.BlockSpec(tile, fwd), pl.BlockSpec(tile, fwd), pl.BlockSpec(tile, bwd)),
        scratch_shapes=[pltpu.VMEM((2 * tq, 1), F32), pltpu.VMEM((2 * tq, V_AUG), F32),
                        pltpu.VMEM((HGRN_DV, HGRN_DK), F32), pltpu.VMEM((HGRN_DV, HGRN_DK), F32)],
    )
    return pl.pallas_call(
        functools.partial(_mixer_kernel, nk=nk),
        out_shape=(jax.ShapeDtypeStruct((b, s, A_HEADS * V_DIM), BF16),
                   jax.ShapeDtypeStruct((b, s, SEG), F32),
                   jax.ShapeDtypeStruct((b, s, SEG), F32)),
        grid_spec=grid_spec,
        compiler_params=pltpu.CompilerParams(
            dimension_semantics=("arbitrary", "arbitrary", "arbitrary"),
            vmem_limit_bytes=VMEM_LIMIT),
        name="mixer",
    )(p["fast"], p["lam"], qa, ka, va, p["bias_tiles"], p["subln_w"],
      qh, ffw, ih, qh, fbw, ih, p["lb"])


def _hgrn_group(q_ref, f_ref, v_ref, o_ref, st, lb, row0, g_chunks, reverse):
    rows_g = g_chunks * CHUNK
    r = lax.broadcasted_iota(jnp.int32, (CHUNK, CHUNK), 0)
    c = lax.broadcasted_iota(jnp.int32, (CHUNK, CHUNK), 1)
    pos = lax.broadcasted_iota(jnp.int32, (rows_g, LANES), 0) % CHUNK
    nt = (((1,), (1,)), ((), ()))
    tn = (((0,), (0,)), ((), ()))
    rows = pl.ds(row0, rows_g)
    q = q_ref[0, rows, :].astype(F32)
    v = v_ref[0, rows, :]
    t = jnp.tanh(0.5 * f_ref[0, rows, :])
    half = 0.5 * (1.0 - lb)
    f = (lb + half) + half * t
    k = half * (1.0 - t)
    b = jnp.log(f)
    step = 1
    while step < CHUNK:
        if reverse:
            b = b + jnp.where(pos < CHUNK - step, pltpu.roll(b, rows_g - step, 0), 0.0)
        else:
            b = b + jnp.where(pos >= step, pltpu.roll(b, step, 0), 0.0)
        step *= 2
    last_row = 0 if reverse else CHUNK - 1
    mask = (r <= c) if reverse else (r >= c)
    b3 = b.reshape(g_chunks, CHUNK, LANES)
    b_last = b3[:, last_row:last_row + 1, :]
    decay = jnp.exp(b_last)
    q_t = (q * jnp.exp(b)).astype(BF16).reshape(g_chunks, CHUNK, LANES)
    k_t = (k * jnp.exp(-b)).astype(BF16).reshape(g_chunks, CHUNK, LANES)
    k_d = (k.reshape(g_chunks, CHUNK, LANES) * jnp.exp(b_last - b3)).astype(BF16)
    v3 = v.reshape(g_chunks, CHUNK, LANES)
    a = [jnp.where(mask, lax.dot_general(q_t[g], k_t[g], nt, preferred_element_type=F32),
                   0.0).astype(BF16) for g in range(g_chunks)]
    ds = [lax.dot_general(v3[g], k_d[g], tn, preferred_element_type=F32)
          for g in range(g_chunks)]
    o_intra = [jnp.dot(a[g], v3[g], preferred_element_type=F32) for g in range(g_chunks)]
    outs = [None] * g_chunks
    for g in (range(g_chunks - 1, -1, -1) if reverse else range(g_chunks)):
        outs[g] = o_intra[g] + lax.dot_general(q_t[g], st.astype(BF16), nt,
                                               preferred_element_type=F32)
        st = decay[g] * st + ds[g]
    o_ref[0, rows, :] = jnp.concatenate(outs, axis=0)
    return st


def _mlp_kernel(x_ref, oa_ref, of_ref, ob_ref, g_ref, hw_ref, wo_ref, mw_ref, w1_ref, w2_ref, y_ref):
    o = of_ref[...] + ob_ref[...]
    g = g_ref[...]
    parts = [oa_ref[...]]
    for hh in range(HGRN_HEADS):
        sl = slice(hh * HGRN_DV, (hh + 1) * HGRN_DV)
        oh = o[:, sl]
        ms = jnp.mean(oh * oh, axis=-1, keepdims=True)
        gh = g[:, sl]
        parts.append((oh * lax.rsqrt(ms + EPS) * hw_ref[...] * _silu(gh)).astype(BF16))
    cat = jnp.concatenate(parts, axis=1)
    x1 = x_ref[...] + jnp.dot(cat, wo_ref[...], preferred_element_type=F32)
    ms = jnp.mean(x1 * x1, axis=-1, keepdims=True)
    h2 = (x1 * lax.rsqrt(ms + EPS) * mw_ref[...]).astype(BF16)
    acc = x1
    for cc in range(D_FF // D_MODEL):
        sl = slice(cc * D_MODEL, (cc + 1) * D_MODEL)
        u = jnp.dot(h2, w1_ref[:, sl], preferred_element_type=F32)
        a = jnp.square(jnp.maximum(u, 0.0)).astype(BF16)
        acc = acc + jnp.dot(a, w2_ref[sl, :], preferred_element_type=F32)
    y_ref[...] = acc


def _mlp_call(x2d, oa, o_f, o_b, gh, p):
    t = x2d.shape[0]
    tm = TM_MLP
    assert t % tm == 0
    row = lambda i: (i, 0)
    const = lambda i: (0, 0)
    once = pl.Buffered(1)
    return pl.pallas_call(
        _mlp_kernel,
        out_shape=jax.ShapeDtypeStruct((t, D_MODEL), F32),
        grid=(t // tm,),
        in_specs=[
            pl.BlockSpec((tm, D_MODEL), row),
            pl.BlockSpec((tm, SEG), row),
            pl.BlockSpec((tm, SEG), row),
            pl.BlockSpec((tm, SEG), row),
            pl.BlockSpec((tm, SEG), row),
            pl.BlockSpec((1, HGRN_DV), const),
            pl.BlockSpec((D_MODEL, D_MODEL), const, pipeline_mode=once),
            pl.BlockSpec((1, D_MODEL), const),
            pl.BlockSpec((D_MODEL, D_FF), const, pipeline_mode=once),
            pl.BlockSpec((D_FF, D_MODEL), const, pipeline_mode=once),
        ],
        out_specs=pl.BlockSpec((tm, D_MODEL), row),
        compiler_params=pltpu.CompilerParams(
            dimension_semantics=("arbitrary",), vmem_limit_bytes=VMEM_LIMIT),
        name="out_mlp",
    )(x2d, oa, o_f, o_b, gh, p["hgrn_norm_w"], p["w_out"], p["mlp_norm_w"], p["w_mlp_in"], p["w_mlp_out"])


def _prepare(attn_norm_w, w_in, qk_norm_w, diff_lambda, diff_subln_w, rel_bias, hgrn_lb,
             hgrn_norm_w, w_out, mlp_norm_w, w_mlp_in, w_mlp_out):
    l = 0
    lam_p = diff_lambda[l].astype(F32)
    lam = jnp.exp(jnp.sum(lam_p[0] * lam_p[1])) - jnp.exp(jnp.sum(lam_p[2] * lam_p[3])) + LAM_INIT
    lb = jnp.cumsum(jax.nn.softmax(hgrn_lb.astype(F32), axis=1), axis=1)[:, l]

    rb = rel_bias.astype(F32)
    left = jnp.take(rb, _t5_bucket(jnp.array([BAND_OFFSETS[0] * TK])), axis=0)
    table = (rb - left) * LOG2E
    reach = max(-BAND_OFFSETS[0] * TK + TQ - 1, BAND_OFFSETS[-1] * TK + TK - 1)
    width = 2 * pl.next_power_of_2(reach + 1)
    u = jnp.arange(width)
    line_buckets = _t5_bucket(jnp.where(u < width // 2, u, u - width))[None].astype(jnp.int32)
    tiles = _bias_call(table, line_buckets)

    wq_eff = qk_norm_w[l, 0].astype(F32) * (QK_DIM ** -0.5 * LOG2E)
    wk_eff = qk_norm_w[l, 1].astype(F32)
    logit_bound = 1.02 * QK_DIM * jnp.max(jnp.abs(wq_eff * wk_eff)) + 2.0 * jnp.max(jnp.abs(table))
    fast = (logit_bound <= FAST_EXP_RANGE).astype(jnp.int32).reshape(1)

    reps = SEG // QK_DIM
    grp = jnp.arange(GROUP_MAT) // QK_DIM
    group_mat = jnp.where(grp[:, None] == grp[None, :], 1.0 / QK_DIM, 0.0).astype(BF16)
    return dict(
        attn_norm_w=attn_norm_w[l][None].astype(F32),
        w_in=w_in[l].astype(BF16),
        group_mat=group_mat,
        wq=jnp.tile(wq_eff, reps)[None],
        wk=jnp.tile(wk_eff, reps)[None],
        fast=fast,
        lam=jnp.reshape(lam, (1,)).astype(F32),
        bias_tiles=tiles,
        subln_w=diff_subln_w[l][None].astype(F32),
        lb=lb,
        hgrn_norm_w=hgrn_norm_w[l][None].astype(F32),
        w_out=w_out[l].astype(BF16),
        mlp_norm_w=mlp_norm_w[l][None].astype(F32),
        w_mlp_in=w_mlp_in[l].astype(BF16),
        w_mlp_out=w_mlp_out[l].astype(BF16),
    )


def _layer(x, p):
    b, s, d = x.shape
    x2d = x.reshape(b * s, d)
    qa, ka, va, qh, ffw, fbw, ih, gh = _proj_call(x2d, p)
    r3 = lambda t: t.reshape(b, s, t.shape[-1])
    oa, o_f, o_b = _mixer_call(r3(qa), r3(ka), r3(va), r3(qh), r3(ffw), r3(fbw), r3(ih), p)
    y = _mlp_call(x2d, oa.reshape(b * s, -1), o_f.reshape(b * s, -1), o_b.reshape(b * s, -1), gh, p)
    return y.reshape(b, s, d)


def kernel(x_prompt, x_sample, attn_norm_w, w_in, qk_norm_w, diff_lambda, diff_subln_w, rel_bias,
           hgrn_lb, hgrn_norm_w, w_out, mlp_norm_w, w_mlp_in, w_mlp_out):
    p = _prepare(attn_norm_w, w_in, qk_norm_w, diff_lambda, diff_subln_w, rel_bias, hgrn_lb,
                 hgrn_norm_w, w_out, mlp_norm_w, w_mlp_in, w_mlp_out)
    return _layer(x_prompt, p), _layer(x_sample, p)
```

```python
import functools
import math

import jax
import jax.numpy as jnp
from jax import lax
from jax.experimental import pallas as pl
from jax.experimental.pallas import tpu as pltpu

F32 = jnp.float32
BF16 = jnp.bfloat16

D_MODEL = 1024
A_HEADS = 4
V_DIM = 128
QK_DIM = 64
HGRN_HEADS = 4
HGRN_DK = 128
HGRN_DV = 128
SEG = 512
N_SEG = 8
D_FF = 4 * D_MODEL
N_BUCKETS = 32
MAX_DISTANCE = 128
CHUNK = 64
EPS = 1e-6
LOG2E = math.log2(math.e)
LAM_INIT = 0.8 - 0.6 * math.exp(-0.3 * 0)

LANES = 128
V_AUG = 2 * LANES
GROUP_MAT = 2 * LANES
VMEM_LIMIT = 56 * 1024 * 1024

TM_PROJ = 1024
TM_MLP = 512
TQ = 1024
TK = 512
KEY_GROUP = 8
BAND_OFFSETS = tuple(range(-2, TQ // TK + 2))
NEG_BIG = -1e30
FAST_EXP_RANGE = 60.0


def _silu(x):
    return (0.5 * x) * (1.0 + jnp.tanh(0.5 * x))


def _t5_bucket(rel):
    nb = N_BUCKETS // 2
    max_exact = nb // 2
    ret = jnp.where(rel > 0, nb, 0)
    n = jnp.abs(rel)
    nf = jnp.maximum(n, 1).astype(jnp.float32)
    large = max_exact + (jnp.log(nf / max_exact) / math.log(MAX_DISTANCE / max_exact)
                         * (nb - max_exact)).astype(jnp.int32)
    large = jnp.minimum(large, nb - 1)
    return ret + jnp.where(n < max_exact, n, large)


def _proj_kernel(x_ref, nw_ref, w_ref, g_ref, wq_ref, wk_ref,
                 qa_ref, ka_ref, va_ref, qh_ref, ffw_ref, fbw_ref, ih_ref, gh_ref):
    x = x_ref[...]
    ms = jnp.mean(x * x, axis=-1, keepdims=True)
    h = (x * lax.rsqrt(ms + EPS) * nw_ref[...]).astype(BF16)

    def seg(i):
        return jnp.dot(h, w_ref[:, i * SEG:(i + 1) * SEG], preferred_element_type=F32)

    def group_norm(t, w):
        sq = (t * t).astype(BF16)
        half = g_ref.shape[0]
        ss = jnp.concatenate(
            [jnp.dot(sq[:, c0:c0 + half], g_ref[...], preferred_element_type=F32)
             for c0 in range(0, SEG, half)], axis=1)
        return (t * lax.rsqrt(ss + EPS) * w).astype(BF16)

    qa_ref[...] = group_norm(seg(0), wq_ref[...])
    ka_ref[...] = group_norm(seg(1), wk_ref[...])

    v = seg(2).astype(BF16)
    ones = jnp.ones((v.shape[0], V_AUG - V_DIM), BF16)
    for hh in range(A_HEADS):
        va_ref[:, hh * V_AUG:hh * V_AUG + V_DIM] = v[:, hh * V_DIM:(hh + 1) * V_DIM]
        va_ref[:, hh * V_AUG + V_DIM:(hh + 1) * V_AUG] = ones

    qh = seg(3)
    qh_ref[...] = (_silu(qh) * (HGRN_DK ** -0.5)).astype(BF16)
    ffw_ref[...] = seg(4)
    fbw_ref[...] = seg(5)
    ih_ref[...] = seg(6).astype(BF16)
    gh_ref[...] = seg(7)


def _proj_call(x2d, p):
    t = x2d.shape[0]
    tm = TM_PROJ
    assert t % tm == 0
    row = lambda i: (i, 0)
    const = lambda i: (0, 0)
    out_shape = (
        jax.ShapeDtypeStruct((t, SEG), BF16),
        jax.ShapeDtypeStruct((t, SEG), BF16),
        jax.ShapeDtypeStruct((t, A_HEADS * V_AUG), BF16),
        jax.ShapeDtypeStruct((t, SEG), BF16),
        jax.ShapeDtypeStruct((t, SEG), F32),
        jax.ShapeDtypeStruct((t, SEG), F32),
        jax.ShapeDtypeStruct((t, SEG), BF16),
        jax.ShapeDtypeStruct((t, SEG), F32),
    )
    out_specs = tuple(pl.BlockSpec((tm, s.shape[1]), row) for s in out_shape)
    return pl.pallas_call(
        _proj_kernel,
        out_shape=out_shape,
        grid=(t // tm,),
        in_specs=[
            pl.BlockSpec((tm, D_MODEL), row),
            pl.BlockSpec((1, D_MODEL), const),
            pl.BlockSpec((D_MODEL, N_SEG * SEG), const, pipeline_mode=pl.Buffered(1)),
            pl.BlockSpec((GROUP_MAT, GROUP_MAT), const),
            pl.BlockSpec((1, SEG), const),
            pl.BlockSpec((1, SEG), const),
        ],
        out_specs=out_specs,
        compiler_params=pltpu.CompilerParams(
            dimension_semantics=("arbitrary",), vmem_limit_bytes=VMEM_LIMIT),
        name="proj",
    )(x2d, p["attn_norm_w"], p["w_in"], p["group_mat"], p["wq"], p["wk"])


def _bias_kernel(tab_ref, bucket_ref, o_ref):
    hh = pl.program_id(0)
    _, n_d, tq, tk = o_ref.shape
    width = bucket_ref.shape[1]
    rows = 64
    bucket = bucket_ref[...]
    line = jnp.zeros(bucket.shape, F32)
    for bb in range(N_BUCKETS):
        line = jnp.where(bucket == bb, tab_ref[bb, hh], line)
    spread = jnp.broadcast_to(line, (rows, width))

    def body(i, carry):
        r0 = pl.multiple_of(i * rows, rows)
        rot = pltpu.roll(spread, r0, 1, stride=1, stride_axis=0)
        for d_idx, d in enumerate(BAND_OFFSETS):
            c0 = (d * tk) % width
            o_ref[0, d_idx, pl.ds(r0, rows), :] = rot[:, c0:c0 + tk]
        return carry

    lax.fori_loop(0, tq // rows, body, 0)


def _bias_call(table, line_buckets):
    n_d = len(BAND_OFFSETS)
    width = line_buckets.shape[1]
    return pl.pallas_call(
        _bias_kernel,
        out_shape=jax.ShapeDtypeStruct((A_HEADS, n_d, TQ, TK), F32),
        grid=(A_HEADS,),
        in_specs=[
            pl.BlockSpec(memory_space=pltpu.SMEM),
            pl.BlockSpec((1, width), lambda hi: (0, 0)),
        ],
        out_specs=pl.BlockSpec((1, n_d, TQ, TK), lambda hi: (hi, 0, 0, 0)),
        compiler_params=pltpu.CompilerParams(
            dimension_semantics=("arbitrary",), vmem_limit_bytes=VMEM_LIMIT),
        name="bias_tiles",
    )(table, line_buckets)


def _mixer_kernel(fast_ref, lam_ref, q_ref, k_ref, v_ref, bias_ref, sw_ref,
                  hqf_ref, hff_ref, hvf_ref, hqb_ref, hfb_ref, hvb_ref, lb_ref,
                  o_ref, of_ref, ob_ref, m_ref, acc_ref, sf_ref, sb_ref, *, nk):
    qi = pl.program_id(2)
    tq = q_ref.shape[1]
    tk = bias_ref.shape[3]
    n_trips = nk // KEY_GROUP
    g_chunks = tq // (CHUNK * n_trips)
    rows_g = g_chunks * CHUNK

    @pl.when(qi == 0)
    def _():
        sf_ref[...] = jnp.zeros(sf_ref.shape, F32)
        sb_ref[...] = jnp.zeros(sb_ref.shape, F32)

    lb_f = lb_ref[0:1, :]
    lb_b = lb_ref[1:2, :]

    def hgrn_step(g):
        row_f = pl.multiple_of(g * rows_g, rows_g)
        sf_ref[...] = _hgrn_group(hqf_ref, hff_ref, hvf_ref, of_ref, sf_ref[...], lb_f, row_f,
                                  g_chunks, False)
        row_b = pl.multiple_of((n_trips - 1 - g) * rows_g, rows_g)
        sb_ref[...] = _hgrn_group(hqb_ref, hfb_ref, hvb_ref, ob_ref, sb_ref[...], lb_b, row_b,
                                  g_chunks, True)

    q = q_ref[0]
    lane = lax.broadcasted_iota(jnp.int32, q.shape, 1)
    zero = jnp.zeros_like(q)
    qs = jnp.concatenate([jnp.where(lane < QK_DIM, q, zero),
                          jnp.where(lane >= QK_DIM, q, zero)], axis=0)

    def scores(j):
        k = k_ref[0, pl.ds(pl.multiple_of(j * tk, tk), tk), :]
        return lax.dot_general(qs, k, (((1,), (1,)), ((), ())), preferred_element_type=F32)

    def values(j):
        return v_ref[0, pl.ds(pl.multiple_of(j * tk, tk), tk), :]

    def add_bias(s, bias):
        return (s.reshape(2, tq, tk) + bias[None]).reshape(2 * tq, tk)

    def for_tiles(lo, hi, fn):
        def body(j, carry):
            fn(j)
            return carry
        lax.fori_loop(lo, hi, body, 0)

    def biased_scores(j):
        d_idx = jnp.clip(j - qi * (tq // tk), BAND_OFFSETS[0], BAND_OFFSETS[-1]) - BAND_OFFSETS[0]
        return add_bias(scores(j), bias_ref[0, d_idx])

    @pl.when(fast_ref[0] == 1)
    def _():
        def group(g, first):
            for u in range(KEY_GROUP):
                j = g * KEY_GROUP + u
                p = jnp.exp2(biased_scores(j)).astype(BF16)
                pv = jnp.dot(p, values(j), preferred_element_type=F32)
                if first and u == 0:
                    acc_ref[...] = pv
                else:
                    acc_ref[...] += pv
            hgrn_step(g)

        group(0, True)
        for_tiles(1, n_trips, lambda g: group(g, False))

    @pl.when(fast_ref[0] != 1)
    def _():
        m_ref[...] = jnp.full(m_ref.shape, NEG_BIG, F32)
        acc_ref[...] = jnp.zeros(acc_ref.shape, F32)

        def update(j):
            s = biased_scores(j)
            m_prev = m_ref[...]
            m_new = jnp.maximum(m_prev, jnp.max(s, axis=-1, keepdims=True))
            alpha = jnp.exp2(m_prev - m_new)
            p = jnp.exp2(s - m_new).astype(BF16)
            acc_ref[...] = alpha * acc_ref[...] + jnp.dot(p, values(j), preferred_element_type=F32)
            m_ref[...] = m_new

        for_tiles(0, n_trips, hgrn_step)
        for_tiles(0, nk, update)

    acc = acc_ref[...]
    o1 = acc[:tq, :V_DIM] / acc[:tq, V_DIM:]
    o2 = acc[tq:, :V_DIM] / acc[tq:, V_DIM:]
    o = o1 - lam_ref[0] * o2
    ms = jnp.mean(o * o, axis=-1, keepdims=True)
    o = o * lax.rsqrt(ms + EPS) * sw_ref[...] * (1.0 - LAM_INIT)
    o_ref[0] = o.astype(BF16)


def _mixer_call(qa, ka, va, qh, ffw, fbw, ih, p):
    b, s, _ = qa.shape
    tq, tk = TQ, TK
    assert tq % tk == 0 and s % tq == 0 and A_HEADS == HGRN_HEADS
    nq, nk = s // tq, s // tk
    assert nk % KEY_GROUP == 0 and tq % (CHUNK * (nk // KEY_GROUP)) == 0
    n_d = len(BAND_OFFSETS)
    fwd = lambda hi, bi, qi, *_: (bi, qi, hi)
    bwd = lambda hi, bi, qi, *_: (bi, nq - 1 - qi, hi)
    head_all = lambda hi, bi, qi, *_: (bi, 0, hi)
    tile = (1, tq, LANES)
    grid_spec = pltpu.PrefetchScalarGridSpec(
        num_scalar_prefetch=2,
        grid=(A_HEADS, b, nq),
        in_specs=[
            pl.BlockSpec(tile, fwd),
            pl.BlockSpec((1, s, LANES), head_all),
            pl.BlockSpec((1, s, V_AUG), head_all),
            pl.BlockSpec((1, n_d, tq, tk), lambda hi, bi, qi, *_: (hi, 0, 0, 0),
                         pipeline_mode=pl.Buffered(1)),
            pl.BlockSpec((1, V_DIM), lambda hi, bi, qi, *_: (0, 0)),
            pl.BlockSpec(tile, fwd), pl.BlockSpec(tile, fwd), pl.BlockSpec(tile, fwd),
            pl.BlockSpec(tile, bwd), pl.BlockSpec(tile, bwd), pl.BlockSpec(tile, bwd),
            pl.BlockSpec((2, LANES), lambda hi, bi, qi, *_: (0, hi)),
        ],
        out_specs=(pl.BlockSpec(tile, fwd), pl.BlockSpec(tile, fwd), pl.BlockSpec(tile, bwd)),
        scratch_shapes=[pltpu.VMEM((2 * tq, 1), F32), pltpu.VMEM((2 * tq, V_AUG), F32),
                        pltpu.VMEM((HGRN_DV, HGRN_DK), F32), pltpu.VMEM((HGRN_DV, HGRN_DK), F32)],
    )
    return pl.pallas_call(
        functools.partial(_mixer_kernel, nk=nk),
        out_shape=(jax.ShapeDtypeStruct((b, s, A_HEADS * V_DIM), BF16),
                   jax.ShapeDtypeStruct((b, s, SEG), F32),
                   jax.ShapeDtypeStruct((b, s, SEG), F32)),
        grid_spec=grid_spec,
        compiler_params=pltpu.CompilerParams(
            dimension_semantics=("arbitrary", "arbitrary", "arbitrary"),
            vmem_limit_bytes=VMEM_LIMIT),
        name="mixer",
    )(p["fast"], p["lam"], qa, ka, va, p["bias_tiles"], p["subln_w"],
      qh, ffw, ih, qh, fbw, ih, p["lb"])


def _hgrn_group(q_ref, f_ref, v_ref, o_ref, st, lb, row0, g_chunks, reverse):
    rows_g = g_chunks * CHUNK
    r = lax.broadcasted_iota(jnp.int32, (CHUNK, CHUNK), 0)
    c = lax.broadcasted_iota(jnp.int32, (CHUNK, CHUNK), 1)
    pos = lax.broadcasted_iota(jnp.int32, (rows_g, LANES), 0) % CHUNK
    nt = (((1,), (1,)), ((), ()))
    tn = (((0,), (0,)), ((), ()))
    rows = pl.ds(row0, rows_g)
    q = q_ref[0, rows, :].astype(F32)
    v = v_ref[0, rows, :]
    t = jnp.tanh(0.5 * f_ref[0, rows, :])
    half = 0.5 * (1.0 - lb)
    f = (lb + half) + half * t
    k = half * (1.0 - t)
    b = jnp.log(f)
    step = 1
    while step < CHUNK:
        if reverse:
            b = b + jnp.where(pos < CHUNK - step, pltpu.roll(b, rows_g - step, 0), 0.0)
        else:
            b = b + jnp.where(pos >= step, pltpu.roll(b, step, 0), 0.0)
        step *= 2
    last_row = 0 if reverse else CHUNK - 1
    mask = (r <= c) if reverse else (r >= c)
    b3 = b.reshape(g_chunks, CHUNK, LANES)
    b_last = b3[:, last_row:last_row + 1, :]
    decay = jnp.exp(b_last)
    q_t = (q * jnp.exp(b)).astype(BF16).reshape(g_chunks, CHUNK, LANES)
    k_t = (k * jnp.exp(-b)).astype(BF16).reshape(g_chunks, CHUNK, LANES)
    k_d = (k.reshape(g_chunks, CHUNK, LANES) * jnp.exp(b_last - b3)).astype(BF16)
    v3 = v.reshape(g_chunks, CHUNK, LANES)
    a = [jnp.where(mask, lax.dot_general(q_t[g], k_t[g], nt, preferred_element_type=F32),
                   0.0).astype(BF16) for g in range(g_chunks)]
    ds = [lax.dot_general(v3[g], k_d[g], tn, preferred_element_type=F32)
          for g in range(g_chunks)]
    o_intra = [jnp.dot(a[g], v3[g], preferred_element_type=F32) for g in range(g_chunks)]
    outs = [None] * g_chunks
    for g in (range(g_chunks - 1, -1, -1) if reverse else range(g_chunks)):
        outs[g] = o_intra[g] + lax.dot_general(q_t[g], st.astype(BF16), nt,
                                               preferred_element_type=F32)
        st = decay[g] * st + ds[g]
    o_ref[0, rows, :] = jnp.concatenate(outs, axis=0)
    return st


def _mlp_kernel(x_ref, oa_ref, of_ref, ob_ref, g_ref, hw_ref, wo_ref, mw_ref, w1_ref, w2_ref, y_ref):
    o = of_ref[...] + ob_ref[...]
    g = g_ref[...]
    parts = [oa_ref[...]]
    for hh in range(HGRN_HEADS):
        sl = slice(hh * HGRN_DV, (hh + 1) * HGRN_DV)
        oh = o[:, sl]
        ms = jnp.mean(oh * oh, axis=-1, keepdims=True)
        gh = g[:, sl]
        parts.append((oh * lax.rsqrt(ms + EPS) * hw_ref[...] * _silu(gh)).astype(BF16))
    cat = jnp.concatenate(parts, axis=1)
    x1 = x_ref[...] + jnp.dot(cat, wo_ref[...], preferred_element_type=F32)
    ms = jnp.mean(x1 * x1, axis=-1, keepdims=True)
    h2 = (x1 * lax.rsqrt(ms + EPS) * mw_ref[...]).astype(BF16)
    acc = x1
    for cc in range(D_FF // D_MODEL):
        sl = slice(cc * D_MODEL, (cc + 1) * D_MODEL)
        u = jnp.dot(h2, w1_ref[:, sl], preferred_element_type=F32)
        a = jnp.square(jnp.maximum(u, 0.0)).astype(BF16)
        acc = acc + jnp.dot(a, w2_ref[sl, :], preferred_element_type=F32)
    y_ref[...] = acc


def _mlp_call(x2d, oa, o_f, o_b, gh, p):
    t = x2d.shape[0]
    tm = TM_MLP
    assert t % tm == 0
    row = lambda i: (i, 0)
    const = lambda i: (0, 0)
    once = pl.Buffered(1)
    return pl.pallas_call(
        _mlp_kernel,
        out_shape=jax.ShapeDtypeStruct((t, D_MODEL), F32),
        grid=(t // tm,),
        in_specs=[
            pl.BlockSpec((tm, D_MODEL), row),
            pl.BlockSpec((tm, SEG), row),
            pl.BlockSpec((tm, SEG), row),
            pl.BlockSpec((tm, SEG), row),
            pl.BlockSpec((tm, SEG), row),
            pl.BlockSpec((1, HGRN_DV), const),
            pl.BlockSpec((D_MODEL, D_MODEL), const, pipeline_mode=once),
            pl.BlockSpec((1, D_MODEL), const),
            pl.BlockSpec((D_MODEL, D_FF), const, pipeline_mode=once),
            pl.BlockSpec((D_FF, D_MODEL), const, pipeline_mode=once),
        ],
        out_specs=pl.BlockSpec((tm, D_MODEL), row),
        compiler_params=pltpu.CompilerParams(
            dimension_semantics=("arbitrary",), vmem_limit_bytes=VMEM_LIMIT),
        name="out_mlp",
    )(x2d, oa, o_f, o_b, gh, p["hgrn_norm_w"], p["w_out"], p["mlp_norm_w"], p["w_mlp_in"], p["w_mlp_out"])


def _prepare(attn_norm_w, w_in, qk_norm_w, diff_lambda, diff_subln_w, rel_bias, hgrn_lb,
             hgrn_norm_w, w_out, mlp_norm_w, w_mlp_in, w_mlp_out):
    l = 0
    lam_p = diff_lambda[l].astype(F32)
    lam = jnp.exp(jnp.sum(lam_p[0] * lam_p[1])) - jnp.exp(jnp.sum(lam_p[2] * lam_p[3])) + LAM_INIT
    lb = jnp.cumsum(jax.nn.softmax(hgrn_lb.astype(F32), axis=1), axis=1)[:, l]

    rb = rel_bias.astype(F32)
    left = jnp.take(rb, _t5_bucket(jnp.array([BAND_OFFSETS[0] * TK])), axis=0)
    table = (rb - left) * LOG2E
    reach = max(-BAND_OFFSETS[0] * TK + TQ - 1, BAND_OFFSETS[-1] * TK + TK - 1)
    width = 2 * pl.next_power_of_2(reach + 1)
    u = jnp.arange(width)
    line_buckets = _t5_bucket(jnp.where(u < width // 2, u, u - width))[None].astype(jnp.int32)
    tiles = _bias_call(table, line_buckets)

    wq_eff = qk_norm_w[l, 0].astype(F32) * (QK_DIM ** -0.5 * LOG2E)
    wk_eff = qk_norm_w[l, 1].astype(F32)
    logit_bound = 1.02 * QK_DIM * jnp.max(jnp.abs(wq_eff * wk_eff)) + 2.0 * jnp.max(jnp.abs(table))
    fast = (logit_bound <= FAST_EXP_RANGE).astype(jnp.int32).reshape(1)

    reps = SEG // QK_DIM
    grp = jnp.arange(GROUP_MAT) // QK_DIM
    group_mat = jnp.where(grp[:, None] == grp[None, :], 1.0 / QK_DIM, 0.0).astype(BF16)
    return dict(
        attn_norm_w=attn_norm_w[l][None].astype(F32),
        w_in=w_in[l].astype(BF16),
        group_mat=group_mat,
        wq=jnp.tile(wq_eff, reps)[None],
        wk=jnp.tile(wk_eff, reps)[None],
        fast=fast,
        lam=jnp.reshape(lam, (1,)).astype(F32),
        bias_tiles=tiles,
        subln_w=diff_subln_w[l][None].astype(F32),
        lb=lb,
        hgrn_norm_w=hgrn_norm_w[l][None].astype(F32),
        w_out=w_out[l].astype(BF16),
        mlp_norm_w=mlp_norm_w[l][None].astype(F32),
        w_mlp_in=w_mlp_in[l].astype(BF16),
        w_mlp_out=w_mlp_out[l].astype(BF16),
    )


def _layer(x, p):
    b, s, d = x.shape
    x2d = x.reshape(b * s, d)
    qa, ka, va, qh, ffw, fbw, ih, gh = _proj_call(x2d, p)
    r3 = lambda t: t.reshape(b, s, t.shape[-1])
    oa, o_f, o_b = _mixer_call(r3(qa), r3(ka), r3(va), r3(qh), r3(ffw), r3(fbw), r3(ih), p)
    y = _mlp_call(x2d, oa.reshape(b * s, -1), o_f.reshape(b * s, -1), o_b.reshape(b * s, -1), gh, p)
    return y.reshape(b, s, d)


def kernel(x_prompt, x_sample, attn_norm_w, w_in, qk_norm_w, diff_lambda, diff_subln_w, rel_bias,
           hgrn_lb, hgrn_norm_w, w_out, mlp_norm_w, w_mlp_in, w_mlp_out):
    p = _prepare(attn_norm_w, w_in, qk_norm_w, diff_lambda, diff_subln_w, rel_bias, hgrn_lb,
                 hgrn_norm_w, w_out, mlp_norm_w, w_mlp_in, w_mlp_out)
    return _layer(x_prompt, p), _layer(x_sample, p)
```

```python
import functools
import math

import jax
import jax.numpy as jnp
from jax import lax
from jax.experimental import pallas as pl
from jax.experimental.pallas import tpu as pltpu

F32 = jnp.float32
BF16 = jnp.bfloat16

D_MODEL = 1024
A_HEADS = 4
V_DIM = 128
QK_DIM = 64
HGRN_HEADS = 4
HGRN_DK = 128
HGRN_DV = 128
SEG = 512
N_SEG = 8
D_FF = 4 * D_MODEL
N_BUCKETS = 32
MAX_DISTANCE = 128
CHUNK = 128
EPS = 1e-6
LOG2E = math.log2(math.e)
LAM_INIT = 0.8 - 0.6 * math.exp(-0.3 * 0)

LANES = 128
V_AUG = 2 * LANES
GROUP_MAT = 2 * LANES
VMEM_LIMIT = 56 * 1024 * 1024

TM_PROJ = 1024
TM_MLP = 512
TQ = 1024
TK = 512
KEY_GROUP = 8
BAND_OFFSETS = tuple(range(-2, TQ // TK + 2))
assert TK >= MAX_DISTANCE and TQ % TK == 0
NEG_BIG = -1e30
FAST_EXP_RANGE = 60.0


def _silu(x):
    return (0.5 * x) * (1.0 + jnp.tanh(0.5 * x))


def _t5_bucket(rel):
    nb = N_BUCKETS // 2
    max_exact = nb // 2
    ret = jnp.where(rel > 0, nb, 0)
    n = jnp.abs(rel)
    nf = jnp.maximum(n, 1).astype(jnp.float32)
    large = max_exact + (jnp.log(nf / max_exact) / math.log(MAX_DISTANCE / max_exact)
                         * (nb - max_exact)).astype(jnp.int32)
    large = jnp.minimum(large, nb - 1)
    return ret + jnp.where(n < max_exact, n, large)


def _proj_kernel(x_ref, nw_ref, w_ref, g_ref, wq_ref, wk_ref,
                 qa_ref, ka_ref, va_ref, qh_ref, ffw_ref, fbw_ref, ih_ref, gh_ref):
    x = x_ref[...]
    ms = jnp.mean(x * x, axis=-1, keepdims=True)
    h = (x * lax.rsqrt(ms + EPS) * nw_ref[...]).astype(BF16)

    def seg(i):
        return jnp.dot(h, w_ref[:, i * SEG:(i + 1) * SEG], preferred_element_type=F32)

    def group_norm(t, w):
        sq = (t * t).astype(BF16)
        half = g_ref.shape[0]
        ss = jnp.concatenate(
            [jnp.dot(sq[:, c0:c0 + half], g_ref[...], preferred_element_type=F32)
             for c0 in range(0, SEG, half)], axis=1)
        return (t * lax.rsqrt(ss + EPS) * w).astype(BF16)

    qa_ref[...] = group_norm(seg(0), wq_ref[...])
    ka_ref[...] = group_norm(seg(1), wk_ref[...])

    v = seg(2).astype(BF16)
    ones = jnp.ones((v.shape[0], V_AUG - V_DIM), BF16)
    for hh in range(A_HEADS):
        va_ref[:, hh * V_AUG:hh * V_AUG + V_DIM] = v[:, hh * V_DIM:(hh + 1) * V_DIM]
        va_ref[:, hh * V_AUG + V_DIM:(hh + 1) * V_AUG] = ones

    qh = seg(3)
    qh_ref[...] = (_silu(qh) * (HGRN_DK ** -0.5)).astype(BF16)
    ffw_ref[...] = seg(4)
    fbw_ref[...] = seg(5)
    ih_ref[...] = seg(6).astype(BF16)
    gh_ref[...] = seg(7)


def _proj_call(x2d, p):
    t = x2d.shape[0]
    tm = TM_PROJ
    assert t % tm == 0
    row = lambda i: (i, 0)
    const = lambda i: (0, 0)
    out_shape = (
        jax.ShapeDtypeStruct((t, SEG), BF16),
        jax.ShapeDtypeStruct((t, SEG), BF16),
        jax.ShapeDtypeStruct((t, A_HEADS * V_AUG), BF16),
        jax.ShapeDtypeStruct((t, SEG), BF16),
        jax.ShapeDtypeStruct((t, SEG), F32),
        jax.ShapeDtypeStruct((t, SEG), F32),
        jax.ShapeDtypeStruct((t, SEG), BF16),
        jax.ShapeDtypeStruct((t, SEG), F32),
    )
    out_specs = tuple(pl.BlockSpec((tm, s.shape[1]), row) for s in out_shape)
    return pl.pallas_call(
        _proj_kernel,
        out_shape=out_shape,
        grid=(t // tm,),
        in_specs=[
            pl.BlockSpec((tm, D_MODEL), row),
            pl.BlockSpec((1, D_MODEL), const),
            pl.BlockSpec((D_MODEL, N_SEG * SEG), const, pipeline_mode=pl.Buffered(1)),
            pl.BlockSpec((GROUP_MAT, GROUP_MAT), const),
            pl.BlockSpec((1, SEG), const),
            pl.BlockSpec((1, SEG), const),
        ],
        out_specs=out_specs,
        compiler_params=pltpu.CompilerParams(
            dimension_semantics=("arbitrary",), vmem_limit_bytes=VMEM_LIMIT),
        name="proj",
    )(x2d, p["attn_norm_w"], p["w_in"], p["group_mat"], p["wq"], p["wk"])


def _bias_kernel(tab_ref, bucket_ref, o_ref):
    hh = pl.program_id(0)
    _, n_d, tq, tk = o_ref.shape
    width = bucket_ref.shape[1]
    rows = 64
    bucket = bucket_ref[...]
    line = jnp.zeros(bucket.shape, F32)
    for bb in range(N_BUCKETS):
        line = jnp.where(bucket == bb, tab_ref[bb, hh], line)
    spread = jnp.broadcast_to(line, (rows, width))

    def body(i, carry):
        r0 = pl.multiple_of(i * rows, rows)
        rot = pltpu.roll(spread, r0, 1, stride=1, stride_axis=0)
        for d_idx, d in enumerate(BAND_OFFSETS):
            c0 = (d * tk) % width
            o_ref[0, d_idx, pl.ds(r0, rows), :] = rot[:, c0:c0 + tk]
        return carry

    lax.fori_loop(0, tq // rows, body, 0)


def _bias_call(table, line_buckets):
    n_d = len(BAND_OFFSETS)
    width = line_buckets.shape[1]
    return pl.pallas_call(
        _bias_kernel,
        out_shape=jax.ShapeDtypeStruct((A_HEADS, n_d, TQ, TK), F32),
        grid=(A_HEADS,),
        in_specs=[
            pl.BlockSpec(memory_space=pltpu.SMEM),
            pl.BlockSpec((1, width), lambda hi: (0, 0)),
        ],
        out_specs=pl.BlockSpec((1, n_d, TQ, TK), lambda hi: (hi, 0, 0, 0)),
        compiler_params=pltpu.CompilerParams(
            dimension_semantics=("arbitrary",), vmem_limit_bytes=VMEM_LIMIT),
        name="bias_tiles",
    )(table, line_buckets)


def _mixer_kernel(fast_ref, lam_ref, q_ref, k_ref, v_ref, bias_ref, sw_ref,
                  hqf_ref, hff_ref, hvf_ref, hqb_ref, hfb_ref, hvb_ref, lb_ref,
                  o_ref, of_ref, ob_ref, m_ref, acc_ref, sf_ref, sb_ref, *, nk):
    qi = pl.program_id(2)
    tq = q_ref.shape[1]
    tk = bias_ref.shape[3]
    n_trips = nk // KEY_GROUP
    g_chunks = tq // (CHUNK * n_trips)
    rows_g = g_chunks * CHUNK

    @pl.when(qi == 0)
    def _():
        sf_ref[...] = jnp.zeros(sf_ref.shape, F32)
        sb_ref[...] = jnp.zeros(sb_ref.shape, F32)

    lb_f = lb_ref[0:1, :]
    lb_b = lb_ref[1:2, :]

    def hgrn_step(g):
        row_f = pl.multiple_of(g * rows_g, rows_g)
        sf_ref[...] = _hgrn_group(hqf_ref, hff_ref, hvf_ref, of_ref, sf_ref[...], lb_f, row_f,
                                  g_chunks, False)
        row_b = pl.multiple_of((n_trips - 1 - g) * rows_g, rows_g)
        sb_ref[...] = _hgrn_group(hqb_ref, hfb_ref, hvb_ref, ob_ref, sb_ref[...], lb_b, row_b,
                                  g_chunks, True)

    q = q_ref[0]
    lane = lax.broadcasted_iota(jnp.int32, q.shape, 1)
    zero = jnp.zeros_like(q)
    qs = jnp.concatenate([jnp.where(lane < QK_DIM, q, zero),
                          jnp.where(lane >= QK_DIM, q, zero)], axis=0)

    def scores(j):
        k = k_ref[0, pl.ds(pl.multiple_of(j * tk, tk), tk), :]
        return lax.dot_general(qs, k, (((1,), (1,)), ((), ())), preferred_element_type=F32)

    def values(j):
        return v_ref[0, pl.ds(pl.multiple_of(j * tk, tk), tk), :]

    def add_bias(s, bias):
        return (s.reshape(2, tq, tk) + bias[None]).reshape(2 * tq, tk)

    def for_tiles(lo, hi, fn):
        def body(j, carry):
            fn(j)
            return carry
        lax.fori_loop(lo, hi, body, 0)

    def biased_scores(j):
        d_idx = jnp.clip(j - qi * (tq // tk), BAND_OFFSETS[0], BAND_OFFSETS[-1]) - BAND_OFFSETS[0]
        return add_bias(scores(j), bias_ref[0, d_idx])

    @pl.when(fast_ref[0] == 1)
    def _():
        def group(g, first):
            for u in range(KEY_GROUP):
                j = g * KEY_GROUP + u
                p = jnp.exp2(biased_scores(j)).astype(BF16)
                pv = jnp.dot(p, values(j), preferred_element_type=F32)
                if first and u == 0:
                    acc_ref[...] = pv
                else:
                    acc_ref[...] += pv
            hgrn_step(g)

        group(0, True)
        for_tiles(1, n_trips, lambda g: group(g, False))

    @pl.when(fast_ref[0] != 1)
    def _():
        m_ref[...] = jnp.full(m_ref.shape, NEG_BIG, F32)
        acc_ref[...] = jnp.zeros(acc_ref.shape, F32)

        def update(j):
            s = biased_scores(j)
            m_prev = m_ref[...]
            m_new = jnp.maximum(m_prev, jnp.max(s, axis=-1, keepdims=True))
            alpha = jnp.exp2(m_prev - m_new)
            p = jnp.exp2(s - m_new).astype(BF16)
            acc_ref[...] = alpha * acc_ref[...] + jnp.dot(p, values(j), preferred_element_type=F32)
            m_ref[...] = m_new

        for_tiles(0, n_trips, hgrn_step)
        for_tiles(0, nk, update)

    acc = acc_ref[...]
    o1 = acc[:tq, :V_DIM] / acc[:tq, V_DIM:]
    o2 = acc[tq:, :V_DIM] / acc[tq:, V_DIM:]
    o = o1 - lam_ref[0] * o2
    ms = jnp.mean(o * o, axis=-1, keepdims=True)
    o = o * lax.rsqrt(ms + EPS) * sw_ref[...] * (1.0 - LAM_INIT)
    o_ref[0] = o.astype(BF16)


def _mixer_call(qa, ka, va, qh, ffw, fbw, ih, p):
    b, s, _ = qa.shape
    tq, tk = TQ, TK
    assert tq % tk == 0 and s % tq == 0 and A_HEADS == HGRN_HEADS
    nq, nk = s // tq, s // tk
    assert nk % KEY_GROUP == 0 and tq % (CHUNK * (nk // KEY_GROUP)) == 0
    n_d = len(BAND_OFFSETS)
    fwd = lambda hi, bi, qi, *_: (bi, qi, hi)
    bwd = lambda hi, bi, qi, *_: (bi, nq - 1 - qi, hi)
    head_all = lambda hi, bi, qi, *_: (bi, 0, hi)
    tile = (1, tq, LANES)
    grid_spec = pltpu.PrefetchScalarGridSpec(
        num_scalar_prefetch=2,
        grid=(A_HEADS, b, nq),
        in_specs=[
            pl.BlockSpec(tile, fwd),
            pl.BlockSpec((1, s, LANES), head_all),
            pl.BlockSpec((1, s, V_AUG), head_all),
            pl.BlockSpec((1, n_d, tq, tk), lambda hi, bi, qi, *_: (hi, 0, 0, 0),
                         pipeline_mode=pl.Buffered(1)),
            pl.BlockSpec((1, V_DIM), lambda hi, bi, qi, *_: (0, 0)),
            pl.BlockSpec(tile, fwd), pl.BlockSpec(tile, fwd), pl.BlockSpec(tile, fwd),
            pl.BlockSpec(tile, bwd), pl.BlockSpec(tile, bwd), pl.BlockSpec(tile, bwd),
            pl.BlockSpec((2, LANES), lambda hi, bi, qi, *_: (0, hi)),
        ],
        out_specs=(pl.BlockSpec(tile, fwd), pl.BlockSpec(tile, fwd), pl.BlockSpec(tile, bwd)),
        scratch_shapes=[pltpu.VMEM((2 * tq, 1), F32), pltpu.VMEM((2 * tq, V_AUG), F32),
                        pltpu.VMEM((HGRN_DV, HGRN_DK), F32), pltpu.VMEM((HGRN_DV, HGRN_DK), F32)],
    )
    return pl.pallas_call(
        functools.partial(_mixer_kernel, nk=nk),
        out_shape=(jax.ShapeDtypeStruct((b, s, A_HEADS * V_DIM), BF16),
                   jax.ShapeDtypeStruct((b, s, SEG), F32),
                   jax.ShapeDtypeStruct((b, s, SEG), F32)),
        grid_spec=grid_spec,
        compiler_params=pltpu.CompilerParams(
            dimension_semantics=("arbitrary", "arbitrary", "arbitrary"),
            vmem_limit_bytes=VMEM_LIMIT),
        name="mixer",
    )(p["fast"], p["lam"], qa, ka, va, p["bias_tiles"], p["subln_w"],
      qh, ffw, ih, qh, fbw, ih, p["lb"])


def _hgrn_group(q_ref, f_ref, v_ref, o_ref, st, lb, row0, g_chunks, reverse):
    rows_g = g_chunks * CHUNK
    r = lax.broadcasted_iota(jnp.int32, (CHUNK, CHUNK), 0)
    c = lax.broadcasted_iota(jnp.int32, (CHUNK, CHUNK), 1)
    pos = lax.broadcasted_iota(jnp.int32, (rows_g, LANES), 0) % CHUNK
    nt = (((1,), (1,)), ((), ()))
    tn = (((0,), (0,)), ((), ()))
    rows = pl.ds(row0, rows_g)
    q = q_ref[0, rows, :].astype(F32)
    v = v_ref[0, rows, :]
    t = jnp.tanh(0.5 * f_ref[0, rows, :])
    half = 0.5 * (1.0 - lb)
    f = (lb + half) + half * t
    k = half * (1.0 - t)
    b = jnp.log(f)
    step = 1
    while step < CHUNK:
        if reverse:
            b = b + jnp.where(pos < CHUNK - step, pltpu.roll(b, rows_g - step, 0), 0.0)
        else:
            b = b + jnp.where(pos >= step, pltpu.roll(b, step, 0), 0.0)
        step *= 2
    last_row = 0 if reverse else CHUNK - 1
    mid_row = CHUNK // 2 if reverse else CHUNK // 2 - 1
    mask = (r <= c) if reverse else (r >= c)
    q3 = q.reshape(g_chunks, CHUNK, LANES)
    k3 = k.reshape(g_chunks, CHUNK, LANES)
    b3 = b.reshape(g_chunks, CHUNK, LANES)
    b_last = b3[:, last_row:last_row + 1, :]
    b_mid = b3[:, mid_row:mid_row + 1, :]
    decay = jnp.exp(b_last)
    q_t = (q3 * jnp.exp(b3 - b_mid)).astype(BF16)
    k_t = (k3 * jnp.exp(b_mid - b3)).astype(BF16)
    q_i = (q3 * jnp.exp(b3)).astype(BF16)
    k_d = (k3 * jnp.exp(b_last - b3)).astype(BF16)
    v3 = v.reshape(g_chunks, CHUNK, LANES)
    a = [jnp.where(mask, lax.dot_general(q_t[g], k_t[g], nt, preferred_element_type=F32),
                   0.0).astype(BF16) for g in range(g_chunks)]
    ds = [lax.dot_general(v3[g], k_d[g], tn, preferred_element_type=F32)
          for g in range(g_chunks)]
    o_intra = [jnp.dot(a[g], v3[g], preferred_element_type=F32) for g in range(g_chunks)]
    outs = [None] * g_chunks
    for g in (range(g_chunks - 1, -1, -1) if reverse else range(g_chunks)):
        outs[g] = o_intra[g] + lax.dot_general(q_i[g], st.astype(BF16), nt,
                                               preferred_element_type=F32)
        st = decay[g] * st + ds[g]
    o_ref[0, rows, :] = jnp.concatenate(outs, axis=0)
    return st


def _mlp_kernel(x_ref, oa_ref, of_ref, ob_ref, g_ref, hw_ref, wo_ref, mw_ref, w1_ref, w2_ref, y_ref):
    o = of_ref[...] + ob_ref[...]
    g = g_ref[...]
    parts = [oa_ref[...]]
    for hh in range(HGRN_HEADS):
        sl = slice(hh * HGRN_DV, (hh + 1) * HGRN_DV)
        oh = o[:, sl]
        ms = jnp.mean(oh * oh, axis=-1, keepdims=True)
        gh = g[:, sl]
        parts.append((oh * lax.rsqrt(ms + EPS) * hw_ref[...] * _silu(gh)).astype(BF16))
    cat = jnp.concatenate(parts, axis=1)
    x1 = x_ref[...] + jnp.dot(cat, wo_ref[...], preferred_element_type=F32)
    ms = jnp.mean(x1 * x1, axis=-1, keepdims=True)
    h2 = (x1 * lax.rsqrt(ms + EPS) * mw_ref[...]).astype(BF16)
    acc = x1
    for cc in range(D_FF // D_MODEL):
        sl = slice(cc * D_MODEL, (cc + 1) * D_MODEL)
        u = jnp.dot(h2, w1_ref[:, sl], preferred_element_type=F32)
        a = jnp.square(jnp.maximum(u, 0.0)).astype(BF16)
        acc = acc + jnp.dot(a, w2_ref[sl, :], preferred_element_type=F32)
    y_ref[...] = acc


def _mlp_call(x2d, oa, o_f, o_b, gh, p):
    t = x2d.shape[0]
    tm = TM_MLP
    assert t % tm == 0
    row = lambda i: (i, 0)
    const = lambda i: (0, 0)
    once = pl.Buffered(1)
    return pl.pallas_call(
        _mlp_kernel,
        out_shape=jax.ShapeDtypeStruct((t, D_MODEL), F32),
        grid=(t // tm,),
        in_specs=[
            pl.BlockSpec((tm, D_MODEL), row),
            pl.BlockSpec((tm, SEG), row),
            pl.BlockSpec((tm, SEG), row),
            pl.BlockSpec((tm, SEG), row),
            pl.BlockSpec((tm, SEG), row),
            pl.BlockSpec((1, HGRN_DV), const),
            pl.BlockSpec((D_MODEL, D_MODEL), const, pipeline_mode=once),
            pl.BlockSpec((1, D_MODEL), const),
            pl.BlockSpec((D_MODEL, D_FF), const, pipeline_mode=once),
            pl.BlockSpec((D_FF, D_MODEL), const, pipeline_mode=once),
        ],
        out_specs=pl.BlockSpec((tm, D_MODEL), row),
        compiler_params=pltpu.CompilerParams(
            dimension_semantics=("arbitrary",), vmem_limit_bytes=VMEM_LIMIT),
        name="out_mlp",
    )(x2d, oa, o_f, o_b, gh, p["hgrn_norm_w"], p["w_out"], p["mlp_norm_w"], p["w_mlp_in"], p["w_mlp_out"])


def _prepare(attn_norm_w, w_in, qk_norm_w, diff_lambda, diff_subln_w, rel_bias, hgrn_lb,
             hgrn_norm_w, w_out, mlp_norm_w, w_mlp_in, w_mlp_out):
    l = 0
    lam_p = diff_lambda[l].astype(F32)
    lam = jnp.exp(jnp.sum(lam_p[0] * lam_p[1])) - jnp.exp(jnp.sum(lam_p[2] * lam_p[3])) + LAM_INIT
    lb = jnp.cumsum(jax.nn.softmax(hgrn_lb.astype(F32), axis=1), axis=1)[:, l]

    rb = rel_bias.astype(F32)
    left = jnp.take(rb, _t5_bucket(jnp.array([BAND_OFFSETS[0] * TK])), axis=0)
    table = (rb - left) * LOG2E
    reach = max(-BAND_OFFSETS[0] * TK + TQ - 1, BAND_OFFSETS[-1] * TK + TK - 1)
    width = 2 * pl.next_power_of_2(reach + 1)
    u = jnp.arange(width)
    line_buckets = _t5_bucket(jnp.where(u < width // 2, u, u - width))[None].astype(jnp.int32)
    tiles = _bias_call(table, line_buckets)

    wq_eff = qk_norm_w[l, 0].astype(F32) * (QK_DIM ** -0.5 * LOG2E)
    wk_eff = qk_norm_w[l, 1].astype(F32)
    logit_bound = 1.02 * QK_DIM * jnp.max(jnp.abs(wq_eff * wk_eff)) + 2.0 * jnp.max(jnp.abs(table))
    fast = (logit_bound <= FAST_EXP_RANGE).astype(jnp.int32).reshape(1)

    reps = SEG // QK_DIM
    grp = jnp.arange(GROUP_MAT) // QK_DIM
    group_mat = jnp.where(grp[:, None] == grp[None, :], 1.0 / QK_DIM, 0.0).astype(BF16)
    return dict(
        attn_norm_w=attn_norm_w[l][None].astype(F32),
        w_in=w_in[l].astype(BF16),
        group_mat=group_mat,
        wq=jnp.tile(wq_eff, reps)[None],
        wk=jnp.tile(wk_eff, reps)[None],
        fast=fast,
        lam=jnp.reshape(lam, (1,)).astype(F32),
        bias_tiles=tiles,
        subln_w=diff_subln_w[l][None].astype(F32),
        lb=lb,
        hgrn_norm_w=hgrn_norm_w[l][None].astype(F32),
        w_out=w_out[l].astype(BF16),
        mlp_norm_w=mlp_norm_w[l][None].astype(F32),
        w_mlp_in=w_mlp_in[l].astype(BF16),
        w_mlp_out=w_mlp_out[l].astype(BF16),
    )


def _layer(x, p):
    b, s, d = x.shape
    x2d = x.reshape(b * s, d)
    qa, ka, va, qh, ffw, fbw, ih, gh = _proj_call(x2d, p)
    r3 = lambda t: t.reshape(b, s, t.shape[-1])
    oa, o_f, o_b = _mixer_call(r3(qa), r3(ka), r3(va), r3(qh), r3(ffw), r3(fbw), r3(ih), p)
    y = _mlp_call(x2d, oa.reshape(b * s, -1), o_f.reshape(b * s, -1), o_b.reshape(b * s, -1), gh, p)
    return y.reshape(b, s, d)


def kernel(x_prompt, x_sample, attn_norm_w, w_in, qk_norm_w, diff_lambda, diff_subln_w, rel_bias,
           hgrn_lb, hgrn_norm_w, w_out, mlp_norm_w, w_mlp_in, w_mlp_out):
    p = _prepare(attn_norm_w, w_in, qk_norm_w, diff_lambda, diff_subln_w, rel_bias, hgrn_lb,
                 hgrn_norm_w, w_out, mlp_norm_w, w_mlp_in, w_mlp_out)
    return _layer(x_prompt, p), _layer(x_sample, p)
```

```python
import functools
import math

import jax
import jax.numpy as jnp
from jax import lax
from jax.experimental import pallas as pl
from jax.experimental.pallas import tpu as pltpu

F32 = jnp.float32
BF16 = jnp.bfloat16

D_MODEL = 1024
A_HEADS = 4
V_DIM = 128
QK_DIM = 64
HGRN_HEADS = 4
HGRN_DK = 128
HGRN_DV = 128
SEG = 512
N_SEG = 8
D_FF = 4 * D_MODEL
N_BUCKETS = 32
MAX_DISTANCE = 128
CHUNK = 128
EPS = 1e-6
LOG2E = math.log2(math.e)
LAM_INIT = 0.8 - 0.6 * math.exp(-0.3 * 0)

LANES = 128
V_AUG = 2 * LANES
GROUP_MAT = 2 * LANES
VMEM_LIMIT = 56 * 1024 * 1024

TM_PROJ = 1024
TM_MLP = 512
TQ = 1024
TK = 512
KEY_GROUP = 8
BAND_OFFSETS = tuple(range(-2, TQ // TK + 2))
assert TK >= MAX_DISTANCE and TQ % TK == 0
BIAS_ROWS = 64
NEG_BIG = -1e30
FAST_EXP_RANGE = 60.0


def _silu(x):
    return (0.5 * x) * (1.0 + jnp.tanh(0.5 * x))


def _t5_bucket(rel):
    nb = N_BUCKETS // 2
    max_exact = nb // 2
    ret = jnp.where(rel > 0, nb, 0)
    n = jnp.abs(rel)
    nf = jnp.maximum(n, 1).astype(jnp.float32)
    large = max_exact + (jnp.log(nf / max_exact) / math.log(MAX_DISTANCE / max_exact)
                         * (nb - max_exact)).astype(jnp.int32)
    large = jnp.minimum(large, nb - 1)
    return ret + jnp.where(n < max_exact, n, large)


def _proj_kernel(x_ref, nw_ref, w_ref, g_ref, wq_ref, wk_ref,
                 qa_ref, ka_ref, va_ref, qh_ref, ffw_ref, fbw_ref, ih_ref, gh_ref):
    x = x_ref[...]
    ms = jnp.mean(x * x, axis=-1, keepdims=True)
    h = (x * lax.rsqrt(ms + EPS) * nw_ref[...]).astype(BF16)

    def seg(i):
        return jnp.dot(h, w_ref[:, i * SEG:(i + 1) * SEG], preferred_element_type=F32)

    def group_norm(t, w):
        sq = (t * t).astype(BF16)
        half = g_ref.shape[0]
        ss = jnp.concatenate(
            [jnp.dot(sq[:, c0:c0 + half], g_ref[...], preferred_element_type=F32)
             for c0 in range(0, SEG, half)], axis=1)
        return (t * lax.rsqrt(ss + EPS) * w).astype(BF16)

    qa_ref[...] = group_norm(seg(0), wq_ref[...])
    ka_ref[...] = group_norm(seg(1), wk_ref[...])

    v = seg(2).astype(BF16)
    ones = jnp.ones((v.shape[0], V_AUG - V_DIM), BF16)
    for hh in range(A_HEADS):
        va_ref[:, hh * V_AUG:hh * V_AUG + V_DIM] = v[:, hh * V_DIM:(hh + 1) * V_DIM]
        va_ref[:, hh * V_AUG + V_DIM:(hh + 1) * V_AUG] = ones

    qh = seg(3)
    qh_ref[...] = (_silu(qh) * (HGRN_DK ** -0.5)).astype(BF16)
    ffw_ref[...] = seg(4)
    fbw_ref[...] = seg(5)
    ih_ref[...] = seg(6).astype(BF16)
    gh_ref[...] = seg(7)


def _proj_call(x2d, p):
    t = x2d.shape[0]
    tm = TM_PROJ
    assert t % tm == 0
    row = lambda i: (i, 0)
    const = lambda i: (0, 0)
    out_shape = (
        jax.ShapeDtypeStruct((t, SEG), BF16),
        jax.ShapeDtypeStruct((t, SEG), BF16),
        jax.ShapeDtypeStruct((t, A_HEADS * V_AUG), BF16),
        jax.ShapeDtypeStruct((t, SEG), BF16),
        jax.ShapeDtypeStruct((t, SEG), F32),
        jax.ShapeDtypeStruct((t, SEG), F32),
        jax.ShapeDtypeStruct((t, SEG), BF16),
        jax.ShapeDtypeStruct((t, SEG), F32),
    )
    out_specs = tuple(pl.BlockSpec((tm, s.shape[1]), row) for s in out_shape)
    return pl.pallas_call(
        _proj_kernel,
        out_shape=out_shape,
        grid=(t // tm,),
        in_specs=[
            pl.BlockSpec((tm, D_MODEL), row),
            pl.BlockSpec((1, D_MODEL), const),
            pl.BlockSpec((D_MODEL, N_SEG * SEG), const, pipeline_mode=pl.Buffered(1)),
            pl.BlockSpec((GROUP_MAT, GROUP_MAT), const),
            pl.BlockSpec((1, SEG), const),
            pl.BlockSpec((1, SEG), const),
        ],
        out_specs=out_specs,
        compiler_params=pltpu.CompilerParams(
            dimension_semantics=("arbitrary",), vmem_limit_bytes=VMEM_LIMIT),
        name="proj",
    )(x2d, p["attn_norm_w"], p["w_in"], p["group_mat"], p["wq"], p["wk"])


def _bias_kernel(tab_ref, bucket_ref, o_ref):
    hh = pl.program_id(0)
    _, n_d, tq, tk = o_ref.shape
    width = bucket_ref.shape[1]
    rows = BIAS_ROWS
    bucket = bucket_ref[...]
    line = jnp.zeros(bucket.shape, F32)
    for bb in range(N_BUCKETS):
        line = jnp.where(bucket == bb, tab_ref[bb, hh], line)
    spread = jnp.broadcast_to(line, (rows, width))

    def body(i, carry):
        r0 = pl.multiple_of(i * rows, rows)
        rot = pltpu.roll(spread, r0, 1, stride=1, stride_axis=0)
        for d_idx, d in enumerate(BAND_OFFSETS):
            c0 = (d * tk) % width
            o_ref[0, d_idx, pl.ds(r0, rows), :] = rot[:, c0:c0 + tk]
        return carry

    lax.fori_loop(0, tq // rows, body, 0)


def _bias_call(table, line_buckets):
    n_d = len(BAND_OFFSETS)
    width = line_buckets.shape[1]
    return pl.pallas_call(
        _bias_kernel,
        out_shape=jax.ShapeDtypeStruct((A_HEADS, n_d, TQ, TK), F32),
        grid=(A_HEADS,),
        in_specs=[
            pl.BlockSpec(memory_space=pltpu.SMEM),
            pl.BlockSpec((1, width), lambda hi: (0, 0)),
        ],
        out_specs=pl.BlockSpec((1, n_d, TQ, TK), lambda hi: (hi, 0, 0, 0)),
        compiler_params=pltpu.CompilerParams(
            dimension_semantics=("arbitrary",), vmem_limit_bytes=VMEM_LIMIT),
        name="bias_tiles",
    )(table, line_buckets)


def _mixer_kernel(fast_ref, lam_ref, q_ref, k_ref, v_ref, bias_ref, sw_ref,
                  hqf_ref, hff_ref, hvf_ref, hqb_ref, hfb_ref, hvb_ref, lb_ref,
                  o_ref, of_ref, ob_ref, m_ref, acc_ref, sf_ref, sb_ref, *, nk):
    qi = pl.program_id(2)
    tq = q_ref.shape[1]
    tk = bias_ref.shape[3]
    n_trips = nk // KEY_GROUP
    g_chunks = tq // (CHUNK * n_trips)
    rows_g = g_chunks * CHUNK

    @pl.when(qi == 0)
    def _():
        sf_ref[...] = jnp.zeros(sf_ref.shape, F32)
        sb_ref[...] = jnp.zeros(sb_ref.shape, F32)

    lb_f = lb_ref[0:1, :]
    lb_b = lb_ref[1:2, :]

    def hgrn_step(g):
        row_f = pl.multiple_of(g * rows_g, rows_g)
        sf_ref[...] = _hgrn_group(hqf_ref, hff_ref, hvf_ref, of_ref, sf_ref[...], lb_f, row_f,
                                  g_chunks, False)
        row_b = pl.multiple_of((n_trips - 1 - g) * rows_g, rows_g)
        sb_ref[...] = _hgrn_group(hqb_ref, hfb_ref, hvb_ref, ob_ref, sb_ref[...], lb_b, row_b,
                                  g_chunks, True)

    q = q_ref[0]
    lane = lax.broadcasted_iota(jnp.int32, q.shape, 1)
    zero = jnp.zeros_like(q)
    qs = jnp.concatenate([jnp.where(lane < QK_DIM, q, zero),
                          jnp.where(lane >= QK_DIM, q, zero)], axis=0)

    def scores(j):
        k = k_ref[0, pl.ds(pl.multiple_of(j * tk, tk), tk), :]
        return lax.dot_general(qs, k, (((1,), (1,)), ((), ())), preferred_element_type=F32)

    def values(j):
        return v_ref[0, pl.ds(pl.multiple_of(j * tk, tk), tk), :]

    def add_bias(s, bias):
        return (s.reshape(2, tq, tk) + bias[None]).reshape(2 * tq, tk)

    def for_tiles(lo, hi, fn):
        def body(j, carry):
            fn(j)
            return carry
        lax.fori_loop(lo, hi, body, 0)

    def biased_scores(j):
        d_idx = jnp.clip(j - qi * (tq // tk), BAND_OFFSETS[0], BAND_OFFSETS[-1]) - BAND_OFFSETS[0]
        return add_bias(scores(j), bias_ref[0, d_idx])

    @pl.when(fast_ref[0] == 1)
    def _():
        def group(g, first):
            for u in range(KEY_GROUP):
                j = g * KEY_GROUP + u
                p = jnp.exp2(biased_scores(j)).astype(BF16)
                pv = jnp.dot(p, values(j), preferred_element_type=F32)
                if first and u == 0:
                    acc_ref[...] = pv
                else:
                    acc_ref[...] += pv
            hgrn_step(g)

        group(0, True)
        for_tiles(1, n_trips, lambda g: group(g, False))

    @pl.when(fast_ref[0] != 1)
    def _():
        m_ref[...] = jnp.full(m_ref.shape, NEG_BIG, F32)
        acc_ref[...] = jnp.zeros(acc_ref.shape, F32)

        def update(j):
            s = biased_scores(j)
            m_prev = m_ref[...]
            m_new = jnp.maximum(m_prev, jnp.max(s, axis=-1, keepdims=True))
            alpha = jnp.exp2(m_prev - m_new)
            p = jnp.exp2(s - m_new).astype(BF16)
            acc_ref[...] = alpha * acc_ref[...] + jnp.dot(p, values(j), preferred_element_type=F32)
            m_ref[...] = m_new

        for_tiles(0, n_trips, hgrn_step)
        for_tiles(0, nk, update)

    acc = acc_ref[...]
    o1 = acc[:tq, :V_DIM] / acc[:tq, V_DIM:]
    o2 = acc[tq:, :V_DIM] / acc[tq:, V_DIM:]
    o = o1 - lam_ref[0] * o2
    ms = jnp.mean(o * o, axis=-1, keepdims=True)
    o = o * lax.rsqrt(ms + EPS) * sw_ref[...] * (1.0 - LAM_INIT)
    o_ref[0] = o.astype(BF16)


def _mixer_call(qa, ka, va, qh, ffw, fbw, ih, p):
    b, s, _ = qa.shape
    tq, tk = TQ, TK
    assert tq % tk == 0 and s % tq == 0 and A_HEADS == HGRN_HEADS
    nq, nk = s // tq, s // tk
    assert nk % KEY_GROUP == 0 and tq % (CHUNK * (nk // KEY_GROUP)) == 0
    n_d = len(BAND_OFFSETS)
    fwd = lambda hi, bi, qi, *_: (bi, qi, hi)
    bwd = lambda hi, bi, qi, *_: (bi, nq - 1 - qi, hi)
    head_all = lambda hi, bi, qi, *_: (bi, 0, hi)
    tile = (1, tq, LANES)
    grid_spec = pltpu.PrefetchScalarGridSpec(
        num_scalar_prefetch=2,
        grid=(A_HEADS, b, nq),
        in_specs=[
            pl.BlockSpec(tile, fwd),
            pl.BlockSpec((1, s, LANES), head_all),
            pl.BlockSpec((1, s, V_AUG), head_all),
            pl.BlockSpec((1, n_d, tq, tk), lambda hi, bi, qi, *_: (hi, 0, 0, 0),
                         pipeline_mode=pl.Buffered(1)),
            pl.BlockSpec((1, V_DIM), lambda hi, bi, qi, *_: (0, 0)),
            pl.BlockSpec(tile, fwd), pl.BlockSpec(tile, fwd), pl.BlockSpec(tile, fwd),
            pl.BlockSpec(tile, bwd), pl.BlockSpec(tile, bwd), pl.BlockSpec(tile, bwd),
            pl.BlockSpec((2, LANES), lambda hi, bi, qi, *_: (0, hi)),
        ],
        out_specs=(pl.BlockSpec(tile, fwd), pl.BlockSpec(tile, fwd), pl.BlockSpec(tile, bwd)),
        scratch_shapes=[pltpu.VMEM((2 * tq, 1), F32), pltpu.VMEM((2 * tq, V_AUG), F32),
                        pltpu.VMEM((HGRN_DV, HGRN_DK), F32), pltpu.VMEM((HGRN_DV, HGRN_DK), F32)],
    )
    return pl.pallas_call(
        functools.partial(_mixer_kernel, nk=nk),
        out_shape=(jax.ShapeDtypeStruct((b, s, A_HEADS * V_DIM), BF16),
                   jax.ShapeDtypeStruct((b, s, SEG), F32),
                   jax.ShapeDtypeStruct((b, s, SEG), F32)),
        grid_spec=grid_spec,
        compiler_params=pltpu.CompilerParams(
            dimension_semantics=("arbitrary", "arbitrary", "arbitrary"),
            vmem_limit_bytes=VMEM_LIMIT),
        name="mixer",
    )(p["fast"], p["lam"], qa, ka, va, p["bias_tiles"], p["subln_w"],
      qh, ffw, ih, qh, fbw, ih, p["lb"])


def _hgrn_group(q_ref, f_ref, v_ref, o_ref, st, lb, row0, g_chunks, reverse):
    rows_g = g_chunks * CHUNK
    r = lax.broadcasted_iota(jnp.int32, (CHUNK, CHUNK), 0)
    c = lax.broadcasted_iota(jnp.int32, (CHUNK, CHUNK), 1)
    pos = lax.broadcasted_iota(jnp.int32, (rows_g, LANES), 0) % CHUNK
    nt = (((1,), (1,)), ((), ()))
    tn = (((0,), (0,)), ((), ()))
    rows = pl.ds(row0, rows_g)
    q = q_ref[0, rows, :].astype(F32)
    v = v_ref[0, rows, :]
    t = jnp.tanh(0.5 * f_ref[0, rows, :])
    half = 0.5 * (1.0 - lb)
    f = (lb + half) + half * t
    k = half * (1.0 - t)
    b = jnp.log(f)
    step = 1
    while step < CHUNK:
        if reverse:
            b = b + jnp.where(pos < CHUNK - step, pltpu.roll(b, rows_g - step, 0), 0.0)
        else:
            b = b + jnp.where(pos >= step, pltpu.roll(b, step, 0), 0.0)
        step *= 2
    last_row = 0 if reverse else CHUNK - 1
    mid_row = CHUNK // 2 if reverse else CHUNK // 2 - 1
    mask = (r <= c) if reverse else (r >= c)
    q3 = q.reshape(g_chunks, CHUNK, LANES)
    k3 = k.reshape(g_chunks, CHUNK, LANES)
    b3 = b.reshape(g_chunks, CHUNK, LANES)
    b_last = b3[:, last_row:last_row + 1, :]
    b_mid = b3[:, mid_row:mid_row + 1, :]
    decay = jnp.exp(b_last)
    q_t = (q3 * jnp.exp(b3 - b_mid)).astype(BF16)
    k_t = (k3 * jnp.exp(b_mid - b3)).astype(BF16)
    q_i = (q3 * jnp.exp(b3)).astype(BF16)
    k_d = (k3 * jnp.exp(b_last - b3)).astype(BF16)
    v3 = v.reshape(g_chunks, CHUNK, LANES)
    a = [jnp.where(mask, lax.dot_general(q_t[g], k_t[g], nt, preferred_element_type=F32),
                   0.0).astype(BF16) for g in range(g_chunks)]
    ds = [lax.dot_general(v3[g], k_d[g], tn, preferred_element_type=F32)
          for g in range(g_chunks)]
    o_intra = [jnp.dot(a[g], v3[g], preferred_element_type=F32) for g in range(g_chunks)]
    outs = [None] * g_chunks
    for g in (range(g_chunks - 1, -1, -1) if reverse else range(g_chunks)):
        outs[g] = o_intra[g] + lax.dot_general(q_i[g], st.astype(BF16), nt,
                                               preferred_element_type=F32)
        st = decay[g] * st + ds[g]
    o_ref[0, rows, :] = jnp.concatenate(outs, axis=0)
    return st


def _mlp_kernel(x_ref, oa_ref, of_ref, ob_ref, g_ref, hw_ref, wo_ref, mw_ref, w1_ref, w2_ref, y_ref):
    o = of_ref[...] + ob_ref[...]
    g = g_ref[...]
    parts = [oa_ref[...]]
    for hh in range(HGRN_HEADS):
        sl = slice(hh * HGRN_DV, (hh + 1) * HGRN_DV)
        oh = o[:, sl]
        ms = jnp.mean(oh * oh, axis=-1, keepdims=True)
        gh = g[:, sl]
        parts.append((oh * lax.rsqrt(ms + EPS) * hw_ref[...] * _silu(gh)).astype(BF16))
    cat = jnp.concatenate(parts, axis=1)
    x1 = x_ref[...] + jnp.dot(cat, wo_ref[...], preferred_element_type=F32)
    ms = jnp.mean(x1 * x1, axis=-1, keepdims=True)
    h2 = (x1 * lax.rsqrt(ms + EPS) * mw_ref[...]).astype(BF16)
    acc = x1
    for cc in range(D_FF // D_MODEL):
        sl = slice(cc * D_MODEL, (cc + 1) * D_MODEL)
        u = jnp.dot(h2, w1_ref[:, sl], preferred_element_type=F32)
        a = jnp.square(jnp.maximum(u, 0.0)).astype(BF16)
        acc = acc + jnp.dot(a, w2_ref[sl, :], preferred_element_type=F32)
    y_ref[...] = acc


def _mlp_call(x2d, oa, o_f, o_b, gh, p):
    t = x2d.shape[0]
    tm = TM_MLP
    assert t % tm == 0
    row = lambda i: (i, 0)
    const = lambda i: (0, 0)
    once = pl.Buffered(1)
    return pl.pallas_call(
        _mlp_kernel,
        out_shape=jax.ShapeDtypeStruct((t, D_MODEL), F32),
        grid=(t // tm,),
        in_specs=[
            pl.BlockSpec((tm, D_MODEL), row),
            pl.BlockSpec((tm, SEG), row),
            pl.BlockSpec((tm, SEG), row),
            pl.BlockSpec((tm, SEG), row),
            pl.BlockSpec((tm, SEG), row),
            pl.BlockSpec((1, HGRN_DV), const),
            pl.BlockSpec((D_MODEL, D_MODEL), const, pipeline_mode=once),
            pl.BlockSpec((1, D_MODEL), const),
            pl.BlockSpec((D_MODEL, D_FF), const, pipeline_mode=once),
            pl.BlockSpec((D_FF, D_MODEL), const, pipeline_mode=once),
        ],
        out_specs=pl.BlockSpec((tm, D_MODEL), row),
        compiler_params=pltpu.CompilerParams(
            dimension_semantics=("arbitrary",), vmem_limit_bytes=VMEM_LIMIT),
        name="out_mlp",
    )(x2d, oa, o_f, o_b, gh, p["hgrn_norm_w"], p["w_out"], p["mlp_norm_w"], p["w_mlp_in"], p["w_mlp_out"])


def _prepare(attn_norm_w, w_in, qk_norm_w, diff_lambda, diff_subln_w, rel_bias, hgrn_lb,
             hgrn_norm_w, w_out, mlp_norm_w, w_mlp_in, w_mlp_out):
    l = 0
    lam_p = diff_lambda[l].astype(F32)
    lam = jnp.exp(jnp.sum(lam_p[0] * lam_p[1])) - jnp.exp(jnp.sum(lam_p[2] * lam_p[3])) + LAM_INIT
    lb = jnp.cumsum(jax.nn.softmax(hgrn_lb.astype(F32), axis=1), axis=1)[:, l]

    rb = rel_bias.astype(F32)
    left = jnp.take(rb, _t5_bucket(jnp.array([BAND_OFFSETS[0] * TK])), axis=0)
    table = (rb - left) * LOG2E
    reach = max(-BAND_OFFSETS[0] * TK + TQ - 1, BAND_OFFSETS[-1] * TK + TK - 1)
    width = 2 * pl.next_power_of_2(reach + 1)
    u = jnp.arange(width)
    line_buckets = _t5_bucket(jnp.where(u < width // 2, u, u - width))[None].astype(jnp.int32)
    tiles = _bias_call(table, line_buckets)

    wq_eff = qk_norm_w[l, 0].astype(F32) * (QK_DIM ** -0.5 * LOG2E)
    wk_eff = qk_norm_w[l, 1].astype(F32)
    logit_bound = 1.02 * QK_DIM * jnp.max(jnp.abs(wq_eff * wk_eff)) + 2.0 * jnp.max(jnp.abs(table))
    fast = (logit_bound <= FAST_EXP_RANGE).astype(jnp.int32).reshape(1)

    reps = SEG // QK_DIM
    grp = jnp.arange(GROUP_MAT) // QK_DIM
    group_mat = jnp.where(grp[:, None] == grp[None, :], 1.0 / QK_DIM, 0.0).astype(BF16)
    return dict(
        attn_norm_w=attn_norm_w[l][None].astype(F32),
        w_in=w_in[l].astype(BF16),
        group_mat=group_mat,
        wq=jnp.tile(wq_eff, reps)[None],
        wk=jnp.tile(wk_eff, reps)[None],
        fast=fast,
        lam=jnp.reshape(lam, (1,)).astype(F32),
        bias_tiles=tiles,
        subln_w=diff_subln_w[l][None].astype(F32),
        lb=lb,
        hgrn_norm_w=hgrn_norm_w[l][None].astype(F32),
        w_out=w_out[l].astype(BF16),
        mlp_norm_w=mlp_norm_w[l][None].astype(F32),
        w_mlp_in=w_mlp_in[l].astype(BF16),
        w_mlp_out=w_mlp_out[l].astype(BF16),
    )


def _layer(x, p):
    b, s, d = x.shape
    x2d = x.reshape(b * s, d)
    qa, ka, va, qh, ffw, fbw, ih, gh = _proj_call(x2d, p)
    r3 = lambda t: t.reshape(b, s, t.shape[-1])
    oa, o_f, o_b = _mixer_call(r3(qa), r3(ka), r3(va), r3(qh), r3(ffw), r3(fbw), r3(ih), p)
    y = _mlp_call(x2d, oa.reshape(b * s, -1), o_f.reshape(b * s, -1), o_b.reshape(b * s, -1), gh, p)
    return y.reshape(b, s, d)


def kernel(x_prompt, x_sample, attn_norm_w, w_in, qk_norm_w, diff_lambda, diff_subln_w, rel_bias,
           hgrn_lb, hgrn_norm_w, w_out, mlp_norm_w, w_mlp_in, w_mlp_out):
    p = _prepare(attn_norm_w, w_in, qk_norm_w, diff_lambda, diff_subln_w, rel_bias, hgrn_lb,
                 hgrn_norm_w, w_out, mlp_norm_w, w_mlp_in, w_mlp_out)
    return _layer(x_prompt, p), _layer(x_sample, p)
```

```python
import functools
import math

import jax
import jax.numpy as jnp
from jax import lax
from jax.experimental import pallas as pl
from jax.experimental.pallas import tpu as pltpu

F32 = jnp.float32
BF16 = jnp.bfloat16

D_MODEL = 1024
A_HEADS = 4
V_DIM = 128
QK_DIM = 64
HGRN_HEADS = 4
HGRN_DK = 128
HGRN_DV = 128
SEG = 512
N_SEG = 8
D_FF = 4 * D_MODEL
N_BUCKETS = 32
MAX_DISTANCE = 128
CHUNK = 128
EPS = 1e-6
LOG2E = math.log2(math.e)
LAM_INIT = 0.8 - 0.6 * math.exp(-0.3 * 0)

LANES = 128
V_AUG = 2 * LANES
GROUP_MAT = 2 * LANES
VMEM_LIMIT = 56 * 1024 * 1024

TM_PROJ = 1024
TM_MLP = 512
TQ = 1024
TK = 512
KEY_GROUP = 8
BAND_OFFSETS = tuple(range(-2, TQ // TK + 2))
assert TK >= MAX_DISTANCE and TQ % TK == 0
BIAS_ROWS = 64
NEG_BIG = -1e30
FAST_EXP_RANGE = 60.0
BF16_SLACK = 1.02


def _silu(x):
    return (0.5 * x) * (1.0 + jnp.tanh(0.5 * x))


def _t5_bucket(rel):
    nb = N_BUCKETS // 2
    max_exact = nb // 2
    ret = jnp.where(rel > 0, nb, 0)
    n = jnp.abs(rel)
    nf = jnp.maximum(n, 1).astype(jnp.float32)
    large = max_exact + (jnp.log(nf / max_exact) / math.log(MAX_DISTANCE / max_exact)
                         * (nb - max_exact)).astype(jnp.int32)
    large = jnp.minimum(large, nb - 1)
    return ret + jnp.where(n < max_exact, n, large)


def _proj_kernel(x_ref, nw_ref, w_ref, g_ref, wq_ref, wk_ref,
                 qa_ref, ka_ref, va_ref, qh_ref, ffw_ref, fbw_ref, ih_ref, gh_ref):
    x = x_ref[...]
    ms = jnp.mean(x * x, axis=-1, keepdims=True)
    h = (x * lax.rsqrt(ms + EPS) * nw_ref[...]).astype(BF16)

    def seg(i):
        return jnp.dot(h, w_ref[:, i * SEG:(i + 1) * SEG], preferred_element_type=F32)

    def group_norm(t, w):
        sq = (t * t).astype(BF16)
        half = g_ref.shape[0]
        ss = jnp.concatenate(
            [jnp.dot(sq[:, c0:c0 + half], g_ref[...], preferred_element_type=F32)
             for c0 in range(0, SEG, half)], axis=1)
        return (t * lax.rsqrt(ss + EPS) * w).astype(BF16)

    qa_ref[...] = group_norm(seg(0), wq_ref[...])
    ka_ref[...] = group_norm(seg(1), wk_ref[...])

    v = seg(2).astype(BF16)
    ones = jnp.ones((v.shape[0], V_AUG - V_DIM), BF16)
    for hh in range(A_HEADS):
        va_ref[:, hh * V_AUG:hh * V_AUG + V_DIM] = v[:, hh * V_DIM:(hh + 1) * V_DIM]
        va_ref[:, hh * V_AUG + V_DIM:(hh + 1) * V_AUG] = ones

    qh = seg(3)
    qh_ref[...] = (_silu(qh) * (HGRN_DK ** -0.5)).astype(BF16)
    ffw_ref[...] = seg(4)
    fbw_ref[...] = seg(5)
    ih_ref[...] = seg(6).astype(BF16)
    gh_ref[...] = seg(7)


def _proj_call(x2d, p):
    t = x2d.shape[0]
    tm = TM_PROJ
    assert t % tm == 0
    row = lambda i: (i, 0)
    const = lambda i: (0, 0)
    out_shape = (
        jax.ShapeDtypeStruct((t, SEG), BF16),
        jax.ShapeDtypeStruct((t, SEG), BF16),
        jax.ShapeDtypeStruct((t, A_HEADS * V_AUG), BF16),
        jax.ShapeDtypeStruct((t, SEG), BF16),
        jax.ShapeDtypeStruct((t, SEG), F32),
        jax.ShapeDtypeStruct((t, SEG), F32),
        jax.ShapeDtypeStruct((t, SEG), BF16),
        jax.ShapeDtypeStruct((t, SEG), F32),
    )
    out_specs = tuple(pl.BlockSpec((tm, s.shape[1]), row) for s in out_shape)
    return pl.pallas_call(
        _proj_kernel,
        out_shape=out_shape,
        grid=(t // tm,),
        in_specs=[
            pl.BlockSpec((tm, D_MODEL), row),
            pl.BlockSpec((1, D_MODEL), const),
            pl.BlockSpec((D_MODEL, N_SEG * SEG), const, pipeline_mode=pl.Buffered(1)),
            pl.BlockSpec((GROUP_MAT, GROUP_MAT), const),
            pl.BlockSpec((1, SEG), const),
            pl.BlockSpec((1, SEG), const),
        ],
        out_specs=out_specs,
        compiler_params=pltpu.CompilerParams(
            dimension_semantics=("arbitrary",), vmem_limit_bytes=VMEM_LIMIT),
        name="proj",
    )(x2d, p["attn_norm_w"], p["w_in"], p["group_mat"], p["wq"], p["wk"])


def _bias_kernel(tab_ref, bucket_ref, o_ref):
    hh = pl.program_id(0)
    _, n_d, tq, tk = o_ref.shape
    width = bucket_ref.shape[1]
    rows = BIAS_ROWS
    bucket = bucket_ref[...]
    line = jnp.zeros(bucket.shape, F32)
    for bb in range(N_BUCKETS):
        line = jnp.where(bucket == bb, tab_ref[bb, hh], line)
    spread = jnp.broadcast_to(line, (rows, width))

    def body(i, carry):
        r0 = pl.multiple_of(i * rows, rows)
        rot = pltpu.roll(spread, r0, 1, stride=1, stride_axis=0)
        for d_idx, d in enumerate(BAND_OFFSETS):
            c0 = (d * tk) % width
            o_ref[0, d_idx, pl.ds(r0, rows), :] = rot[:, c0:c0 + tk]
        return carry

    lax.fori_loop(0, tq // rows, body, 0)


def _bias_call(table, line_buckets):
    n_d = len(BAND_OFFSETS)
    width = line_buckets.shape[1]
    return pl.pallas_call(
        _bias_kernel,
        out_shape=jax.ShapeDtypeStruct((A_HEADS, n_d, TQ, TK), F32),
        grid=(A_HEADS,),
        in_specs=[
            pl.BlockSpec(memory_space=pltpu.SMEM),
            pl.BlockSpec((1, width), lambda hi: (0, 0)),
        ],
        out_specs=pl.BlockSpec((1, n_d, TQ, TK), lambda hi: (hi, 0, 0, 0)),
        compiler_params=pltpu.CompilerParams(
            dimension_semantics=("arbitrary",), vmem_limit_bytes=VMEM_LIMIT),
        name="bias_tiles",
    )(table, line_buckets)


def _mixer_kernel(fast_ref, lam_ref, q_ref, k_ref, v_ref, bias_ref, sw_ref,
                  hqf_ref, hff_ref, hvf_ref, hqb_ref, hfb_ref, hvb_ref, lb_ref,
                  o_ref, of_ref, ob_ref, m_ref, acc_ref, sf_ref, sb_ref, *, nk):
    qi = pl.program_id(2)
    tq = q_ref.shape[1]
    tk = bias_ref.shape[3]
    n_trips = nk // KEY_GROUP
    g_chunks = tq // (CHUNK * n_trips)
    rows_g = g_chunks * CHUNK

    @pl.when(qi == 0)
    def _():
        sf_ref[...] = jnp.zeros(sf_ref.shape, F32)
        sb_ref[...] = jnp.zeros(sb_ref.shape, F32)

    lb_f = lb_ref[0:1, :]
    lb_b = lb_ref[1:2, :]

    def hgrn_step(g):
        row_f = pl.multiple_of(g * rows_g, rows_g)
        sf_ref[...] = _hgrn_group(hqf_ref, hff_ref, hvf_ref, of_ref, sf_ref[...], lb_f, row_f,
                                  g_chunks, False)
        row_b = pl.multiple_of((n_trips - 1 - g) * rows_g, rows_g)
        sb_ref[...] = _hgrn_group(hqb_ref, hfb_ref, hvb_ref, ob_ref, sb_ref[...], lb_b, row_b,
                                  g_chunks, True)

    q = q_ref[0]
    lane = lax.broadcasted_iota(jnp.int32, q.shape, 1)
    zero = jnp.zeros_like(q)
    qs = jnp.concatenate([jnp.where(lane < QK_DIM, q, zero),
                          jnp.where(lane >= QK_DIM, q, zero)], axis=0)

    def scores(j):
        k = k_ref[0, pl.ds(pl.multiple_of(j * tk, tk), tk), :]
        return lax.dot_general(qs, k, (((1,), (1,)), ((), ())), preferred_element_type=F32)

    def values(j):
        return v_ref[0, pl.ds(pl.multiple_of(j * tk, tk), tk), :]

    def add_bias(s, bias):
        return (s.reshape(2, tq, tk) + bias[None]).reshape(2 * tq, tk)

    def for_tiles(lo, hi, fn):
        def body(j, carry):
            fn(j)
            return carry
        lax.fori_loop(lo, hi, body, 0)

    def biased_scores(j):
        d_idx = jnp.clip(j - qi * (tq // tk), BAND_OFFSETS[0], BAND_OFFSETS[-1]) - BAND_OFFSETS[0]
        return add_bias(scores(j), bias_ref[0, d_idx])

    @pl.when(fast_ref[0] == 1)
    def _():
        def group(g, first):
            for u in range(KEY_GROUP):
                j = g * KEY_GROUP + u
                p = jnp.exp2(biased_scores(j)).astype(BF16)
                pv = jnp.dot(p, values(j), preferred_element_type=F32)
                if first and u == 0:
                    acc_ref[...] = pv
                else:
                    acc_ref[...] += pv
            hgrn_step(g)

        group(0, True)
        for_tiles(1, n_trips, lambda g: group(g, False))

    @pl.when(fast_ref[0] != 1)
    def _():
        m_ref[...] = jnp.full(m_ref.shape, NEG_BIG, F32)
        acc_ref[...] = jnp.zeros(acc_ref.shape, F32)

        def update(j):
            s = biased_scores(j)
            m_prev = m_ref[...]
            m_new = jnp.maximum(m_prev, jnp.max(s, axis=-1, keepdims=True))
            alpha = jnp.exp2(m_prev - m_new)
            p = jnp.exp2(s - m_new).astype(BF16)
            acc_ref[...] = alpha * acc_ref[...] + jnp.dot(p, values(j), preferred_element_type=F32)
            m_ref[...] = m_new

        for_tiles(0, n_trips, hgrn_step)
        for_tiles(0, nk, update)

    acc = acc_ref[...]
    o1 = acc[:tq, :V_DIM] / acc[:tq, V_DIM:]
    o2 = acc[tq:, :V_DIM] / acc[tq:, V_DIM:]
    o = o1 - lam_ref[0] * o2
    ms = jnp.mean(o * o, axis=-1, keepdims=True)
    o = o * lax.rsqrt(ms + EPS) * sw_ref[...] * (1.0 - LAM_INIT)
    o_ref[0] = o.astype(BF16)


def _mixer_call(qa, ka, va, qh, ffw, fbw, ih, p):
    b, s, _ = qa.shape
    tq, tk = TQ, TK
    assert tq % tk == 0 and s % tq == 0 and A_HEADS == HGRN_HEADS
    nq, nk = s // tq, s // tk
    assert nk % KEY_GROUP == 0 and tq % (CHUNK * (nk // KEY_GROUP)) == 0
    n_d = len(BAND_OFFSETS)
    fwd = lambda hi, bi, qi, *_: (bi, qi, hi)
    bwd = lambda hi, bi, qi, *_: (bi, nq - 1 - qi, hi)
    head_all = lambda hi, bi, qi, *_: (bi, 0, hi)
    tile = (1, tq, LANES)
    grid_spec = pltpu.PrefetchScalarGridSpec(
        num_scalar_prefetch=2,
        grid=(A_HEADS, b, nq),
        in_specs=[
            pl.BlockSpec(tile, fwd),
            pl.BlockSpec((1, s, LANES), head_all),
            pl.BlockSpec((1, s, V_AUG), head_all),
            pl.BlockSpec((1, n_d, tq, tk), lambda hi, bi, qi, *_: (hi, 0, 0, 0),
                         pipeline_mode=pl.Buffered(1)),
            pl.BlockSpec((1, V_DIM), lambda hi, bi, qi, *_: (0, 0)),
            pl.BlockSpec(tile, fwd), pl.BlockSpec(tile, fwd), pl.BlockSpec(tile, fwd),
            pl.BlockSpec(tile, bwd), pl.BlockSpec(tile, bwd), pl.BlockSpec(tile, bwd),
            pl.BlockSpec((2, LANES), lambda hi, bi, qi, *_: (0, hi)),
        ],
        out_specs=(pl.BlockSpec(tile, fwd), pl.BlockSpec(tile, fwd), pl.BlockSpec(tile, bwd)),
        scratch_shapes=[pltpu.VMEM((2 * tq, 1), F32), pltpu.VMEM((2 * tq, V_AUG), F32),
                        pltpu.VMEM((HGRN_DV, HGRN_DK), F32), pltpu.VMEM((HGRN_DV, HGRN_DK), F32)],
    )
    return pl.pallas_call(
        functools.partial(_mixer_kernel, nk=nk),
        out_shape=(jax.ShapeDtypeStruct((b, s, A_HEADS * V_DIM), BF16),
                   jax.ShapeDtypeStruct((b, s, SEG), F32),
                   jax.ShapeDtypeStruct((b, s, SEG), F32)),
        grid_spec=grid_spec,
        compiler_params=pltpu.CompilerParams(
            dimension_semantics=("arbitrary", "arbitrary", "arbitrary"),
            vmem_limit_bytes=VMEM_LIMIT),
        name="mixer",
    )(p["fast"], p["lam"], qa, ka, va, p["bias_tiles"], p["subln_w"],
      qh, ffw, ih, qh, fbw, ih, p["lb"])


def _hgrn_group(q_ref, f_ref, v_ref, o_ref, st, lb, row0, g_chunks, reverse):
    rows_g = g_chunks * CHUNK
    r = lax.broadcasted_iota(jnp.int32, (CHUNK, CHUNK), 0)
    c = lax.broadcasted_iota(jnp.int32, (CHUNK, CHUNK), 1)
    pos = lax.broadcasted_iota(jnp.int32, (rows_g, LANES), 0) % CHUNK
    nt = (((1,), (1,)), ((), ()))
    tn = (((0,), (0,)), ((), ()))
    rows = pl.ds(row0, rows_g)
    q = q_ref[0, rows, :].astype(F32)
    v = v_ref[0, rows, :]
    t = jnp.tanh(0.5 * f_ref[0, rows, :])
    half = 0.5 * (1.0 - lb)
    f = (lb + half) + half * t
    k = half * (1.0 - t)
    b = jnp.log(f)
    step = 1
    while step < CHUNK:
        if reverse:
            b = b + jnp.where(pos < CHUNK - step, pltpu.roll(b, rows_g - step, 0), 0.0)
        else:
            b = b + jnp.where(pos >= step, pltpu.roll(b, step, 0), 0.0)
        step *= 2
    last_row = 0 if reverse else CHUNK - 1
    mid_row = CHUNK // 2 if reverse else CHUNK // 2 - 1
    mask = (r <= c) if reverse else (r >= c)
    q3 = q.reshape(g_chunks, CHUNK, LANES)
    k3 = k.reshape(g_chunks, CHUNK, LANES)
    b3 = b.reshape(g_chunks, CHUNK, LANES)
    b_last = b3[:, last_row:last_row + 1, :]
    b_mid = b3[:, mid_row:mid_row + 1, :]
    decay = jnp.exp(b_last)
    q_t = (q3 * jnp.exp(b3 - b_mid)).astype(BF16)
    k_t = (k3 * jnp.exp(b_mid - b3)).astype(BF16)
    q_i = (q3 * jnp.exp(b3)).astype(BF16)
    k_d = (k3 * jnp.exp(b_last - b3)).astype(BF16)
    v3 = v.reshape(g_chunks, CHUNK, LANES)
    a = [jnp.where(mask, lax.dot_general(q_t[g], k_t[g], nt, preferred_element_type=F32),
                   0.0).astype(BF16) for g in range(g_chunks)]
    ds = [lax.dot_general(v3[g], k_d[g], tn, preferred_element_type=F32)
          for g in range(g_chunks)]
    o_intra = [jnp.dot(a[g], v3[g], preferred_element_type=F32) for g in range(g_chunks)]
    outs = [None] * g_chunks
    for g in (range(g_chunks - 1, -1, -1) if reverse else range(g_chunks)):
        outs[g] = o_intra[g] + lax.dot_general(q_i[g], st.astype(BF16), nt,
                                               preferred_element_type=F32)
        st = decay[g] * st + ds[g]
    o_ref[0, rows, :] = jnp.concatenate(outs, axis=0)
    return st


def _mlp_kernel(x_ref, oa_ref, of_ref, ob_ref, g_ref, hw_ref, wo_ref, mw_ref, w1_ref, w2_ref, y_ref):
    o = of_ref[...] + ob_ref[...]
    g = g_ref[...]
    parts = [oa_ref[...]]
    for hh in range(HGRN_HEADS):
        sl = slice(hh * HGRN_DV, (hh + 1) * HGRN_DV)
        oh = o[:, sl]
        ms = jnp.mean(oh * oh, axis=-1, keepdims=True)
        gh = g[:, sl]
        parts.append((oh * lax.rsqrt(ms + EPS) * hw_ref[...] * _silu(gh)).astype(BF16))
    cat = jnp.concatenate(parts, axis=1)
    x1 = x_ref[...] + jnp.dot(cat, wo_ref[...], preferred_element_type=F32)
    ms = jnp.mean(x1 * x1, axis=-1, keepdims=True)
    h2 = (x1 * lax.rsqrt(ms + EPS) * mw_ref[...]).astype(BF16)
    acc = x1
    for cc in range(D_FF // D_MODEL):
        sl = slice(cc * D_MODEL, (cc + 1) * D_MODEL)
        u = jnp.dot(h2, w1_ref[:, sl], preferred_element_type=F32)
        a = jnp.square(jnp.maximum(u, 0.0)).astype(BF16)
        acc = acc + jnp.dot(a, w2_ref[sl, :], preferred_element_type=F32)
    y_ref[...] = acc


def _mlp_call(x2d, oa, o_f, o_b, gh, p):
    t = x2d.shape[0]
    tm = TM_MLP
    assert t % tm == 0
    row = lambda i: (i, 0)
    const = lambda i: (0, 0)
    once = pl.Buffered(1)
    return pl.pallas_call(
        _mlp_kernel,
        out_shape=jax.ShapeDtypeStruct((t, D_MODEL), F32),
        grid=(t // tm,),
        in_specs=[
            pl.BlockSpec((tm, D_MODEL), row),
            pl.BlockSpec((tm, SEG), row),
            pl.BlockSpec((tm, SEG), row),
            pl.BlockSpec((tm, SEG), row),
            pl.BlockSpec((tm, SEG), row),
            pl.BlockSpec((1, HGRN_DV), const),
            pl.BlockSpec((D_MODEL, D_MODEL), const, pipeline_mode=once),
            pl.BlockSpec((1, D_MODEL), const),
            pl.BlockSpec((D_MODEL, D_FF), const, pipeline_mode=once),
            pl.BlockSpec((D_FF, D_MODEL), const, pipeline_mode=once),
        ],
        out_specs=pl.BlockSpec((tm, D_MODEL), row),
        compiler_params=pltpu.CompilerParams(
            dimension_semantics=("arbitrary",), vmem_limit_bytes=VMEM_LIMIT),
        name="out_mlp",
    )(x2d, oa, o_f, o_b, gh, p["hgrn_norm_w"], p["w_out"], p["mlp_norm_w"], p["w_mlp_in"], p["w_mlp_out"])


def _prepare(attn_norm_w, w_in, qk_norm_w, diff_lambda, diff_subln_w, rel_bias, hgrn_lb,
             hgrn_norm_w, w_out, mlp_norm_w, w_mlp_in, w_mlp_out):
    l = 0
    lam_p = diff_lambda[l].astype(F32)
    lam = jnp.exp(jnp.sum(lam_p[0] * lam_p[1])) - jnp.exp(jnp.sum(lam_p[2] * lam_p[3])) + LAM_INIT
    lb = jnp.cumsum(jax.nn.softmax(hgrn_lb.astype(F32), axis=1), axis=1)[:, l]

    rb = rel_bias.astype(F32)
    left = jnp.take(rb, _t5_bucket(jnp.array([BAND_OFFSETS[0] * TK])), axis=0)
    table = (rb - left) * LOG2E
    reach = max(-BAND_OFFSETS[0] * TK + TQ - 1, BAND_OFFSETS[-1] * TK + TK - 1)
    width = 2 * pl.next_power_of_2(reach + 1)
    u = jnp.arange(width)
    line_buckets = _t5_bucket(jnp.where(u < width // 2, u, u - width))[None].astype(jnp.int32)
    tiles = _bias_call(table, line_buckets)

    wq_eff = qk_norm_w[l, 0].astype(F32) * (QK_DIM ** -0.5 * LOG2E)
    wk_eff = qk_norm_w[l, 1].astype(F32)
    logit_bound = (BF16_SLACK * QK_DIM * jnp.max(jnp.abs(wq_eff * wk_eff))
                   + 2.0 * jnp.max(jnp.abs(table)))
    fast = (logit_bound <= FAST_EXP_RANGE).astype(jnp.int32).reshape(1)

    reps = SEG // QK_DIM
    grp = jnp.arange(GROUP_MAT) // QK_DIM
    group_mat = jnp.where(grp[:, None] == grp[None, :], 1.0 / QK_DIM, 0.0).astype(BF16)
    return dict(
        attn_norm_w=attn_norm_w[l][None].astype(F32),
        w_in=w_in[l].astype(BF16),
        group_mat=group_mat,
        wq=jnp.tile(wq_eff, reps)[None],
        wk=jnp.tile(wk_eff, reps)[None],
        fast=fast,
        lam=jnp.reshape(lam, (1,)).astype(F32),
        bias_tiles=tiles,
        subln_w=diff_subln_w[l][None].astype(F32),
        lb=lb,
        hgrn_norm_w=hgrn_norm_w[l][None].astype(F32),
        w_out=w_out[l].astype(BF16),
        mlp_norm_w=mlp_norm_w[l][None].astype(F32),
        w_mlp_in=w_mlp_in[l].astype(BF16),
        w_mlp_out=w_mlp_out[l].astype(BF16),
    )


def _layer(x, p):
    b, s, d = x.shape
    x2d = x.reshape(b * s, d)
    qa, ka, va, qh, ffw, fbw, ih, gh = _proj_call(x2d, p)
    r3 = lambda t: t.reshape(b, s, t.shape[-1])
    oa, o_f, o_b = _mixer_call(r3(qa), r3(ka), r3(va), r3(qh), r3(ffw), r3(fbw), r3(ih), p)
    y = _mlp_call(x2d, oa.reshape(b * s, -1), o_f.reshape(b * s, -1), o_b.reshape(b * s, -1), gh, p)
    return y.reshape(b, s, d)


def kernel(x_prompt, x_sample, attn_norm_w, w_in, qk_norm_w, diff_lambda, diff_subln_w, rel_bias,
           hgrn_lb, hgrn_norm_w, w_out, mlp_norm_w, w_mlp_in, w_mlp_out):
    p = _prepare(attn_norm_w, w_in, qk_norm_w, diff_lambda, diff_subln_w, rel_bias, hgrn_lb,
                 hgrn_norm_w, w_out, mlp_norm_w, w_mlp_in, w_mlp_out)
    return _layer(x_prompt, p), _layer(x_sample, p)
```

```python
import functools
import math

import jax
import jax.numpy as jnp
from jax import lax
from jax.experimental import pallas as pl
from jax.experimental.pallas import tpu as pltpu

F32 = jnp.float32
BF16 = jnp.bfloat16

D_MODEL = 1024
A_HEADS = 4
V_DIM = 128
QK_DIM = 64
HGRN_HEADS = 4
HGRN_DK = 128
HGRN_DV = 128
SEG = 512
N_SEG = 8
D_FF = 4 * D_MODEL
N_BUCKETS = 32
MAX_DISTANCE = 128
CHUNK = 128
EPS = 1e-6
LOG2E = math.log2(math.e)
LAM_INIT = 0.8 - 0.6 * math.exp(-0.3 * 0)

LANES = 128
V_AUG = 2 * LANES
GROUP_MAT = 2 * LANES
VMEM_LIMIT = 56 * 1024 * 1024

TM_PROJ = 1024
TM_MLP = 512
TQ = 1024
TK = 256
KEY_GROUP = 16
BAND_OFFSETS = tuple(range(-2, TQ // TK + 2))
assert TK >= MAX_DISTANCE and TQ % TK == 0
BIAS_ROWS = 64
NEG_BIG = -1e30
FAST_EXP_RANGE = 60.0
BF16_SLACK = 1.02


def _silu(x):
    return (0.5 * x) * (1.0 + jnp.tanh(0.5 * x))


def _t5_bucket(rel):
    nb = N_BUCKETS // 2
    max_exact = nb // 2
    ret = jnp.where(rel > 0, nb, 0)
    n = jnp.abs(rel)
    nf = jnp.maximum(n, 1).astype(jnp.float32)
    large = max_exact + (jnp.log(nf / max_exact) / math.log(MAX_DISTANCE / max_exact)
                         * (nb - max_exact)).astype(jnp.int32)
    large = jnp.minimum(large, nb - 1)
    return ret + jnp.where(n < max_exact, n, large)


def _proj_kernel(x_ref, nw_ref, w_ref, g_ref, wq_ref, wk_ref,
                 qa_ref, ka_ref, va_ref, qh_ref, ffw_ref, fbw_ref, ih_ref, gh_ref):
    x = x_ref[...]
    ms = jnp.mean(x * x, axis=-1, keepdims=True)
    h = (x * lax.rsqrt(ms + EPS) * nw_ref[...]).astype(BF16)

    def seg(i):
        return jnp.dot(h, w_ref[:, i * SEG:(i + 1) * SEG], preferred_element_type=F32)

    def group_norm(t, w):
        sq = (t * t).astype(BF16)
        half = g_ref.shape[0]
        ss = jnp.concatenate(
            [jnp.dot(sq[:, c0:c0 + half], g_ref[...], preferred_element_type=F32)
             for c0 in range(0, SEG, half)], axis=1)
        return (t * lax.rsqrt(ss + EPS) * w).astype(BF16)

    qa_ref[...] = group_norm(seg(0), wq_ref[...])
    ka_ref[...] = group_norm(seg(1), wk_ref[...])

    v = seg(2).astype(BF16)
    ones = jnp.ones((v.shape[0], V_AUG - V_DIM), BF16)
    for hh in range(A_HEADS):
        va_ref[:, hh * V_AUG:hh * V_AUG + V_DIM] = v[:, hh * V_DIM:(hh + 1) * V_DIM]
        va_ref[:, hh * V_AUG + V_DIM:(hh + 1) * V_AUG] = ones

    qh = seg(3)
    qh_ref[...] = (_silu(qh) * (HGRN_DK ** -0.5)).astype(BF16)
    ffw_ref[...] = seg(4)
    fbw_ref[...] = seg(5)
    ih_ref[...] = seg(6).astype(BF16)
    gh_ref[...] = seg(7)


def _proj_call(x2d, p):
    t = x2d.shape[0]
    tm = TM_PROJ
    assert t % tm == 0
    row = lambda i: (i, 0)
    const = lambda i: (0, 0)
    out_shape = (
        jax.ShapeDtypeStruct((t, SEG), BF16),
        jax.ShapeDtypeStruct((t, SEG), BF16),
        jax.ShapeDtypeStruct((t, A_HEADS * V_AUG), BF16),
        jax.ShapeDtypeStruct((t, SEG), BF16),
        jax.ShapeDtypeStruct((t, SEG), F32),
        jax.ShapeDtypeStruct((t, SEG), F32),
        jax.ShapeDtypeStruct((t, SEG), BF16),
        jax.ShapeDtypeStruct((t, SEG), F32),
    )
    out_specs = tuple(pl.BlockSpec((tm, s.shape[1]), row) for s in out_shape)
    return pl.pallas_call(
        _proj_kernel,
        out_shape=out_shape,
        grid=(t // tm,),
        in_specs=[
            pl.BlockSpec((tm, D_MODEL), row),
            pl.BlockSpec((1, D_MODEL), const),
            pl.BlockSpec((D_MODEL, N_SEG * SEG), const, pipeline_mode=pl.Buffered(1)),
            pl.BlockSpec((GROUP_MAT, GROUP_MAT), const),
            pl.BlockSpec((1, SEG), const),
            pl.BlockSpec((1, SEG), const),
        ],
        out_specs=out_specs,
        compiler_params=pltpu.CompilerParams(
            dimension_semantics=("arbitrary",), vmem_limit_bytes=VMEM_LIMIT),
        name="proj",
    )(x2d, p["attn_norm_w"], p["w_in"], p["group_mat"], p["wq"], p["wk"])


def _bias_kernel(tab_ref, bucket_ref, o_ref):
    hh = pl.program_id(0)
    _, n_d, tq, tk = o_ref.shape
    width = bucket_ref.shape[1]
    rows = BIAS_ROWS
    bucket = bucket_ref[...]
    line = jnp.zeros(bucket.shape, F32)
    for bb in range(N_BUCKETS):
        line = jnp.where(bucket == bb, tab_ref[bb, hh], line)
    spread = jnp.broadcast_to(line, (rows, width))

    def body(i, carry):
        r0 = pl.multiple_of(i * rows, rows)
        rot = pltpu.roll(spread, r0, 1, stride=1, stride_axis=0)
        for d_idx, d in enumerate(BAND_OFFSETS):
            c0 = (d * tk) % width
            o_ref[0, d_idx, pl.ds(r0, rows), :] = rot[:, c0:c0 + tk]
        return carry

    lax.fori_loop(0, tq // rows, body, 0)


def _bias_call(table, line_buckets):
    n_d = len(BAND_OFFSETS)
    width = line_buckets.shape[1]
    return pl.pallas_call(
        _bias_kernel,
        out_shape=jax.ShapeDtypeStruct((A_HEADS, n_d, TQ, TK), F32),
        grid=(A_HEADS,),
        in_specs=[
            pl.BlockSpec(memory_space=pltpu.SMEM),
            pl.BlockSpec((1, width), lambda hi: (0, 0)),
        ],
        out_specs=pl.BlockSpec((1, n_d, TQ, TK), lambda hi: (hi, 0, 0, 0)),
        compiler_params=pltpu.CompilerParams(
            dimension_semantics=("arbitrary",), vmem_limit_bytes=VMEM_LIMIT),
        name="bias_tiles",
    )(table, line_buckets)


def _mixer_kernel(fast_ref, lam_ref, q_ref, k_ref, v_ref, bias_ref, sw_ref,
                  hqf_ref, hff_ref, hvf_ref, hqb_ref, hfb_ref, hvb_ref, lb_ref,
                  o_ref, of_ref, ob_ref, m_ref, acc_ref, sf_ref, sb_ref, *, nk):
    qi = pl.program_id(2)
    tq = q_ref.shape[1]
    tk = bias_ref.shape[3]
    n_trips = nk // KEY_GROUP
    g_chunks = tq // (CHUNK * n_trips)
    rows_g = g_chunks * CHUNK

    @pl.when(qi == 0)
    def _():
        sf_ref[...] = jnp.zeros(sf_ref.shape, F32)
        sb_ref[...] = jnp.zeros(sb_ref.shape, F32)

    lb_f = lb_ref[0:1, :]
    lb_b = lb_ref[1:2, :]

    def hgrn_step(g):
        row_f = pl.multiple_of(g * rows_g, rows_g)
        sf_ref[...] = _hgrn_group(hqf_ref, hff_ref, hvf_ref, of_ref, sf_ref[...], lb_f, row_f,
                                  g_chunks, False)
        row_b = pl.multiple_of((n_trips - 1 - g) * rows_g, rows_g)
        sb_ref[...] = _hgrn_group(hqb_ref, hfb_ref, hvb_ref, ob_ref, sb_ref[...], lb_b, row_b,
                                  g_chunks, True)

    q = q_ref[0]
    lane = lax.broadcasted_iota(jnp.int32, q.shape, 1)
    zero = jnp.zeros_like(q)
    qs = jnp.concatenate([jnp.where(lane < QK_DIM, q, zero),
                          jnp.where(lane >= QK_DIM, q, zero)], axis=0)

    def scores(j):
        k = k_ref[0, pl.ds(pl.multiple_of(j * tk, tk), tk), :]
        return lax.dot_general(qs, k, (((1,), (1,)), ((), ())), preferred_element_type=F32)

    def values(j):
        return v_ref[0, pl.ds(pl.multiple_of(j * tk, tk), tk), :]

    def add_bias(s, bias):
        return (s.reshape(2, tq, tk) + bias[None]).reshape(2 * tq, tk)

    def for_tiles(lo, hi, fn):
        def body(j, carry):
            fn(j)
            return carry
        lax.fori_loop(lo, hi, body, 0)

    def biased_scores(j):
        d_idx = jnp.clip(j - qi * (tq // tk), BAND_OFFSETS[0], BAND_OFFSETS[-1]) - BAND_OFFSETS[0]
        return add_bias(scores(j), bias_ref[0, d_idx])

    @pl.when(fast_ref[0] == 1)
    def _():
        def group(g, first):
            for u in range(KEY_GROUP):
                j = g * KEY_GROUP + u
                p = jnp.exp2(biased_scores(j)).astype(BF16)
                pv = jnp.dot(p, values(j), preferred_element_type=F32)
                if first and u == 0:
                    acc_ref[...] = pv
                else:
                    acc_ref[...] += pv
            hgrn_step(g)

        group(0, True)
        for_tiles(1, n_trips, lambda g: group(g, False))

    @pl.when(fast_ref[0] != 1)
    def _():
        m_ref[...] = jnp.full(m_ref.shape, NEG_BIG, F32)
        acc_ref[...] = jnp.zeros(acc_ref.shape, F32)

        def update(j):
            s = biased_scores(j)
            m_prev = m_ref[...]
            m_new = jnp.maximum(m_prev, jnp.max(s, axis=-1, keepdims=True))
            alpha = jnp.exp2(m_prev - m_new)
            p = jnp.exp2(s - m_new).astype(BF16)
            acc_ref[...] = alpha * acc_ref[...] + jnp.dot(p, values(j), preferred_element_type=F32)
            m_ref[...] = m_new

        for_tiles(0, n_trips, hgrn_step)
        for_tiles(0, nk, update)

    acc = acc_ref[...]
    o1 = acc[:tq, :V_DIM] / acc[:tq, V_DIM:]
    o2 = acc[tq:, :V_DIM] / acc[tq:, V_DIM:]
    o = o1 - lam_ref[0] * o2
    ms = jnp.mean(o * o, axis=-1, keepdims=True)
    o = o * lax.rsqrt(ms + EPS) * sw_ref[...] * (1.0 - LAM_INIT)
    o_ref[0] = o.astype(BF16)


def _mixer_call(qa, ka, va, qh, ffw, fbw, ih, p):
    b, s, _ = qa.shape
    tq, tk = TQ, TK
    assert tq % tk == 0 and s % tq == 0 and A_HEADS == HGRN_HEADS
    nq, nk = s // tq, s // tk
    assert nk % KEY_GROUP == 0 and tq % (CHUNK * (nk // KEY_GROUP)) == 0
    n_d = len(BAND_OFFSETS)
    fwd = lambda hi, bi, qi, *_: (bi, qi, hi)
    bwd = lambda hi, bi, qi, *_: (bi, nq - 1 - qi, hi)
    head_all = lambda hi, bi, qi, *_: (bi, 0, hi)
    tile = (1, tq, LANES)
    grid_spec = pltpu.PrefetchScalarGridSpec(
        num_scalar_prefetch=2,
        grid=(A_HEADS, b, nq),
        in_specs=[
            pl.BlockSpec(tile, fwd),
            pl.BlockSpec((1, s, LANES), head_all),
            pl.BlockSpec((1, s, V_AUG), head_all),
            pl.BlockSpec((1, n_d, tq, tk), lambda hi, bi, qi, *_: (hi, 0, 0, 0),
                         pipeline_mode=pl.Buffered(1)),
            pl.BlockSpec((1, V_DIM), lambda hi, bi, qi, *_: (0, 0)),
            pl.BlockSpec(tile, fwd), pl.BlockSpec(tile, fwd), pl.BlockSpec(tile, fwd),
            pl.BlockSpec(tile, bwd), pl.BlockSpec(tile, bwd), pl.BlockSpec(tile, bwd),
            pl.BlockSpec((2, LANES), lambda hi, bi, qi, *_: (0, hi)),
        ],
        out_specs=(pl.BlockSpec(tile, fwd), pl.BlockSpec(tile, fwd), pl.BlockSpec(tile, bwd)),
        scratch_shapes=[pltpu.VMEM((2 * tq, 1), F32), pltpu.VMEM((2 * tq, V_AUG), F32),
                        pltpu.VMEM((HGRN_DV, HGRN_DK), F32), pltpu.VMEM((HGRN_DV, HGRN_DK), F32)],
    )
    return pl.pallas_call(
        functools.partial(_mixer_kernel, nk=nk),
        out_shape=(jax.ShapeDtypeStruct((b, s, A_HEADS * V_DIM), BF16),
                   jax.ShapeDtypeStruct((b, s, SEG), F32),
                   jax.ShapeDtypeStruct((b, s, SEG), F32)),
        grid_spec=grid_spec,
        compiler_params=pltpu.CompilerParams(
            dimension_semantics=("arbitrary", "arbitrary", "arbitrary"),
            vmem_limit_bytes=VMEM_LIMIT),
        name="mixer",
    )(p["fast"], p["lam"], qa, ka, va, p["bias_tiles"], p["subln_w"],
      qh, ffw, ih, qh, fbw, ih, p["lb"])


def _hgrn_group(q_ref, f_ref, v_ref, o_ref, st, lb, row0, g_chunks, reverse):
    rows_g = g_chunks * CHUNK
    r = lax.broadcasted_iota(jnp.int32, (CHUNK, CHUNK), 0)
    c = lax.broadcasted_iota(jnp.int32, (CHUNK, CHUNK), 1)
    pos = lax.broadcasted_iota(jnp.int32, (rows_g, LANES), 0) % CHUNK
    nt = (((1,), (1,)), ((), ()))
    tn = (((0,), (0,)), ((), ()))
    rows = pl.ds(row0, rows_g)
    q = q_ref[0, rows, :].astype(F32)
    v = v_ref[0, rows, :]
    t = jnp.tanh(0.5 * f_ref[0, rows, :])
    half = 0.5 * (1.0 - lb)
    f = (lb + half) + half * t
    k = half * (1.0 - t)
    b = jnp.log(f)
    step = 1
    while step < CHUNK:
        if reverse:
            b = b + jnp.where(pos < CHUNK - step, pltpu.roll(b, rows_g - step, 0), 0.0)
        else:
            b = b + jnp.where(pos >= step, pltpu.roll(b, step, 0), 0.0)
        step *= 2
    last_row = 0 if reverse else CHUNK - 1
    mid_row = CHUNK // 2 if reverse else CHUNK // 2 - 1
    mask = (r <= c) if reverse else (r >= c)
    q3 = q.reshape(g_chunks, CHUNK, LANES)
    k3 = k.reshape(g_chunks, CHUNK, LANES)
    b3 = b.reshape(g_chunks, CHUNK, LANES)
    b_last = b3[:, last_row:last_row + 1, :]
    b_mid = b3[:, mid_row:mid_row + 1, :]
    decay = jnp.exp(b_last)
    q_t = (q3 * jnp.exp(b3 - b_mid)).astype(BF16)
    k_t = (k3 * jnp.exp(b_mid - b3)).astype(BF16)
    q_i = (q3 * jnp.exp(b3)).astype(BF16)
    k_d = (k3 * jnp.exp(b_last - b3)).astype(BF16)
    v3 = v.reshape(g_chunks, CHUNK, LANES)
    a = [jnp.where(mask, lax.dot_general(q_t[g], k_t[g], nt, preferred_element_type=F32),
                   0.0).astype(BF16) for g in range(g_chunks)]
    ds = [lax.dot_general(v3[g], k_d[g], tn, preferred_element_type=F32)
          for g in range(g_chunks)]
    o_intra = [jnp.dot(a[g], v3[g], preferred_element_type=F32) for g in range(g_chunks)]
    outs = [None] * g_chunks
    for g in (range(g_chunks - 1, -1, -1) if reverse else range(g_chunks)):
        outs[g] = o_intra[g] + lax.dot_general(q_i[g], st.astype(BF16), nt,
                                               preferred_element_type=F32)
        st = decay[g] * st + ds[g]
    o_ref[0, rows, :] = jnp.concatenate(outs, axis=0)
    return st


def _mlp_kernel(x_ref, oa_ref, of_ref, ob_ref, g_ref, hw_ref, wo_ref, mw_ref, w1_ref, w2_ref, y_ref):
    o = of_ref[...] + ob_ref[...]
    g = g_ref[...]
    parts = [oa_ref[...]]
    for hh in range(HGRN_HEADS):
        sl = slice(hh * HGRN_DV, (hh + 1) * HGRN_DV)
        oh = o[:, sl]
        ms = jnp.mean(oh * oh, axis=-1, keepdims=True)
        gh = g[:, sl]
        parts.append((oh * lax.rsqrt(ms + EPS) * hw_ref[...] * _silu(gh)).astype(BF16))
    cat = jnp.concatenate(parts, axis=1)
    x1 = x_ref[...] + jnp.dot(cat, wo_ref[...], preferred_element_type=F32)
    ms = jnp.mean(x1 * x1, axis=-1, keepdims=True)
    h2 = (x1 * lax.rsqrt(ms + EPS) * mw_ref[...]).astype(BF16)
    acc = x1
    for cc in range(D_FF // D_MODEL):
        sl = slice(cc * D_MODEL, (cc + 1) * D_MODEL)
        u = jnp.dot(h2, w1_ref[:, sl], preferred_element_type=F32)
        a = jnp.square(jnp.maximum(u, 0.0)).astype(BF16)
        acc = acc + jnp.dot(a, w2_ref[sl, :], preferred_element_type=F32)
    y_ref[...] = acc


def _mlp_call(x2d, oa, o_f, o_b, gh, p):
    t = x2d.shape[0]
    tm = TM_MLP
    assert t % tm == 0
    row = lambda i: (i, 0)
    const = lambda i: (0, 0)
    once = pl.Buffered(1)
    return pl.pallas_call(
        _mlp_kernel,
        out_shape=jax.ShapeDtypeStruct((t, D_MODEL), F32),
        grid=(t // tm,),
        in_specs=[
            pl.BlockSpec((tm, D_MODEL), row),
            pl.BlockSpec((tm, SEG), row),
            pl.BlockSpec((tm, SEG), row),
            pl.BlockSpec((tm, SEG), row),
            pl.BlockSpec((tm, SEG), row),
            pl.BlockSpec((1, HGRN_DV), const),
            pl.BlockSpec((D_MODEL, D_MODEL), const, pipeline_mode=once),
            pl.BlockSpec((1, D_MODEL), const),
            pl.BlockSpec((D_MODEL, D_FF), const, pipeline_mode=once),
            pl.BlockSpec((D_FF, D_MODEL), const, pipeline_mode=once),
        ],
        out_specs=pl.BlockSpec((tm, D_MODEL), row),
        compiler_params=pltpu.CompilerParams(
            dimension_semantics=("arbitrary",), vmem_limit_bytes=VMEM_LIMIT),
        name="out_mlp",
    )(x2d, oa, o_f, o_b, gh, p["hgrn_norm_w"], p["w_out"], p["mlp_norm_w"], p["w_mlp_in"], p["w_mlp_out"])


def _prepare(attn_norm_w, w_in, qk_norm_w, diff_lambda, diff_subln_w, rel_bias, hgrn_lb,
             hgrn_norm_w, w_out, mlp_norm_w, w_mlp_in, w_mlp_out):
    l = 0
    lam_p = diff_lambda[l].astype(F32)
    lam = jnp.exp(jnp.sum(lam_p[0] * lam_p[1])) - jnp.exp(jnp.sum(lam_p[2] * lam_p[3])) + LAM_INIT
    lb = jnp.cumsum(jax.nn.softmax(hgrn_lb.astype(F32), axis=1), axis=1)[:, l]

    rb = rel_bias.astype(F32)
    left = jnp.take(rb, _t5_bucket(jnp.array([BAND_OFFSETS[0] * TK])), axis=0)
    table = (rb - left) * LOG2E
    reach = max(-BAND_OFFSETS[0] * TK + TQ - 1, BAND_OFFSETS[-1] * TK + TK - 1)
    width = 2 * pl.next_power_of_2(reach + 1)
    u = jnp.arange(width)
    line_buckets = _t5_bucket(jnp.where(u < width // 2, u, u - width))[None].astype(jnp.int32)
    tiles = _bias_call(table, line_buckets)

    wq_eff = qk_norm_w[l, 0].astype(F32) * (QK_DIM ** -0.5 * LOG2E)
    wk_eff = qk_norm_w[l, 1].astype(F32)
    logit_bound = (BF16_SLACK * QK_DIM * jnp.max(jnp.abs(wq_eff * wk_eff))
                   + 2.0 * jnp.max(jnp.abs(table)))
    fast = (logit_bound <= FAST_EXP_RANGE).astype(jnp.int32).reshape(1)

    reps = SEG // QK_DIM
    grp = jnp.arange(GROUP_MAT) // QK_DIM
    group_mat = jnp.where(grp[:, None] == grp[None, :], 1.0 / QK_DIM, 0.0).astype(BF16)
    return dict(
        attn_norm_w=attn_norm_w[l][None].astype(F32),
        w_in=w_in[l].astype(BF16),
        group_mat=group_mat,
        wq=jnp.tile(wq_eff, reps)[None],
        wk=jnp.tile(wk_eff, reps)[None],
        fast=fast,
        lam=jnp.reshape(lam, (1,)).astype(F32),
        bias_tiles=tiles,
        subln_w=diff_subln_w[l][None].astype(F32),
        lb=lb,
        hgrn_norm_w=hgrn_norm_w[l][None].astype(F32),
        w_out=w_out[l].astype(BF16),
        mlp_norm_w=mlp_norm_w[l][None].astype(F32),
        w_mlp_in=w_mlp_in[l].astype(BF16),
        w_mlp_out=w_mlp_out[l].astype(BF16),
    )


def _layer(x, p):
    b, s, d = x.shape
    x2d = x.reshape(b * s, d)
    qa, ka, va, qh, ffw, fbw, ih, gh = _proj_call(x2d, p)
    r3 = lambda t: t.reshape(b, s, t.shape[-1])
    oa, o_f, o_b = _mixer_call(r3(qa), r3(ka), r3(va), r3(qh), r3(ffw), r3(fbw), r3(ih), p)
    y = _mlp_call(x2d, oa.reshape(b * s, -1), o_f.reshape(b * s, -1), o_b.reshape(b * s, -1), gh, p)
    return y.reshape(b, s, d)


def kernel(x_prompt, x_sample, attn_norm_w, w_in, qk_norm_w, diff_lambda, diff_subln_w, rel_bias,
           hgrn_lb, hgrn_norm_w, w_out, mlp_norm_w, w_mlp_in, w_mlp_out):
    p = _prepare(attn_norm_w, w_in, qk_norm_w, diff_lambda, diff_subln_w, rel_bias, hgrn_lb,
                 hgrn_norm_w, w_out, mlp_norm_w, w_mlp_in, w_mlp_out)
    return _layer(x_prompt, p), _layer(x_sample, p)
```

```python
import functools
import math

import jax
import jax.numpy as jnp
from jax import lax
from jax.experimental import pallas as pl
from jax.experimental.pallas import tpu as pltpu

F32 = jnp.float32
BF16 = jnp.bfloat16

D_MODEL = 1024
A_HEADS = 4
V_DIM = 128
QK_DIM = 64
HGRN_HEADS = 4
HGRN_DK = 128
HGRN_DV = 128
SEG = 512
N_SEG = 8
D_FF = 4 * D_MODEL
N_BUCKETS = 32
MAX_DISTANCE = 128
CHUNK = 128
EPS = 1e-6
LOG2E = math.log2(math.e)
LAM_INIT = 0.8 - 0.6 * math.exp(-0.3 * 0)

LANES = 128
V_AUG = 2 * LANES
GROUP_MAT = 2 * LANES
VMEM_LIMIT = 56 * 1024 * 1024

TM_PROJ = 1024
TM_MLP = 512
TQ = 1024
TK = 512
KEY_GROUP = 8
BAND_OFFSETS = tuple(range(-2, TQ // TK + 2))
assert TK >= MAX_DISTANCE and TQ % TK == 0
BIAS_ROWS = 64
NEG_BIG = -1e30
FAST_EXP_RANGE = 60.0
BF16_SLACK = 1.02


def _silu(x):
    return (0.5 * x) * (1.0 + jnp.tanh(0.5 * x))


def _t5_bucket(rel):
    nb = N_BUCKETS // 2
    max_exact = nb // 2
    ret = jnp.where(rel > 0, nb, 0)
    n = jnp.abs(rel)
    nf = jnp.maximum(n, 1).astype(jnp.float32)
    large = max_exact + (jnp.log(nf / max_exact) / math.log(MAX_DISTANCE / max_exact)
                         * (nb - max_exact)).astype(jnp.int32)
    large = jnp.minimum(large, nb - 1)
    return ret + jnp.where(n < max_exact, n, large)


def _proj_kernel(x_ref, nw_ref, w_ref, g_ref, wq_ref, wk_ref,
                 qa_ref, ka_ref, va_ref, qh_ref, ffw_ref, fbw_ref, ih_ref, gh_ref):
    x = x_ref[...]
    ms = jnp.mean(x * x, axis=-1, keepdims=True)
    h = (x * lax.rsqrt(ms + EPS) * nw_ref[...]).astype(BF16)

    def seg(i):
        return jnp.dot(h, w_ref[:, i * SEG:(i + 1) * SEG], preferred_element_type=F32)

    def group_norm(t, w):
        sq = (t * t).astype(BF16)
        half = g_ref.shape[0]
        ss = jnp.concatenate(
            [jnp.dot(sq[:, c0:c0 + half], g_ref[...], preferred_element_type=F32)
             for c0 in range(0, SEG, half)], axis=1)
        return (t * lax.rsqrt(ss + EPS) * w).astype(BF16)

    qn = group_norm(seg(0), wq_ref[...])
    first = (lax.broadcasted_iota(jnp.int32, qn.shape, 1) % (2 * QK_DIM)) < QK_DIM
    zero = jnp.zeros_like(qn)
    qa_ref[0] = jnp.where(first, qn, zero)
    qa_ref[1] = jnp.where(first, zero, qn)
    ka_ref[...] = group_norm(seg(1), wk_ref[...])

    v = seg(2).astype(BF16)
    ones = jnp.ones((v.shape[0], V_AUG - V_DIM), BF16)
    for hh in range(A_HEADS):
        va_ref[:, hh * V_AUG:hh * V_AUG + V_DIM] = v[:, hh * V_DIM:(hh + 1) * V_DIM]
        va_ref[:, hh * V_AUG + V_DIM:(hh + 1) * V_AUG] = ones

    qh = seg(3)
    qh_ref[...] = (_silu(qh) * (HGRN_DK ** -0.5)).astype(BF16)
    ffw_ref[...] = seg(4)
    fbw_ref[...] = seg(5)
    ih_ref[...] = seg(6).astype(BF16)
    gh_ref[...] = seg(7)


def _proj_call(x2d, p):
    t = x2d.shape[0]
    tm = TM_PROJ
    assert t % tm == 0
    row = lambda i: (i, 0)
    const = lambda i: (0, 0)
    out_shape = (
        jax.ShapeDtypeStruct((2, t, SEG), BF16),
        jax.ShapeDtypeStruct((t, SEG), BF16),
        jax.ShapeDtypeStruct((t, A_HEADS * V_AUG), BF16),
        jax.ShapeDtypeStruct((t, SEG), BF16),
        jax.ShapeDtypeStruct((t, SEG), F32),
        jax.ShapeDtypeStruct((t, SEG), F32),
        jax.ShapeDtypeStruct((t, SEG), BF16),
        jax.ShapeDtypeStruct((t, SEG), F32),
    )
    out_specs = tuple(pl.BlockSpec((2, tm, SEG), lambda i: (0, i, 0)) if len(s.shape) == 3
                      else pl.BlockSpec((tm, s.shape[1]), row) for s in out_shape)
    return pl.pallas_call(
        _proj_kernel,
        out_shape=out_shape,
        grid=(t // tm,),
        in_specs=[
            pl.BlockSpec((tm, D_MODEL), row),
            pl.BlockSpec((1, D_MODEL), const),
            pl.BlockSpec((D_MODEL, N_SEG * SEG), const, pipeline_mode=pl.Buffered(1)),
            pl.BlockSpec((GROUP_MAT, GROUP_MAT), const),
            pl.BlockSpec((1, SEG), const),
            pl.BlockSpec((1, SEG), const),
        ],
        out_specs=out_specs,
        compiler_params=pltpu.CompilerParams(
            dimension_semantics=("arbitrary",), vmem_limit_bytes=VMEM_LIMIT),
        name="proj",
    )(x2d, p["attn_norm_w"], p["w_in"], p["group_mat"], p["wq"], p["wk"])


def _bias_kernel(tab_ref, bucket_ref, o_ref):
    hh = pl.program_id(0)
    _, n_d, tq, tk = o_ref.shape
    width = bucket_ref.shape[1]
    rows = BIAS_ROWS
    bucket = bucket_ref[...]
    line = jnp.zeros(bucket.shape, F32)
    for bb in range(N_BUCKETS):
        line = jnp.where(bucket == bb, tab_ref[bb, hh], line)
    spread = jnp.broadcast_to(line, (rows, width))

    def body(i, carry):
        r0 = pl.multiple_of(i * rows, rows)
        rot = pltpu.roll(spread, r0, 1, stride=1, stride_axis=0)
        for d_idx, d in enumerate(BAND_OFFSETS):
            c0 = (d * tk) % width
            o_ref[0, d_idx, pl.ds(r0, rows), :] = rot[:, c0:c0 + tk]
        return carry

    lax.fori_loop(0, tq // rows, body, 0)


def _bias_call(table, line_buckets):
    n_d = len(BAND_OFFSETS)
    width = line_buckets.shape[1]
    return pl.pallas_call(
        _bias_kernel,
        out_shape=jax.ShapeDtypeStruct((A_HEADS, n_d, TQ, TK), F32),
        grid=(A_HEADS,),
        in_specs=[
            pl.BlockSpec(memory_space=pltpu.SMEM),
            pl.BlockSpec((1, width), lambda hi: (0, 0)),
        ],
        out_specs=pl.BlockSpec((1, n_d, TQ, TK), lambda hi: (hi, 0, 0, 0)),
        compiler_params=pltpu.CompilerParams(
            dimension_semantics=("arbitrary",), vmem_limit_bytes=VMEM_LIMIT),
        name="bias_tiles",
    )(table, line_buckets)


def _mixer_kernel(fast_ref, lam_ref, q_ref, k_ref, v_ref, bias_ref, sw_ref,
                  hqf_ref, hff_ref, hvf_ref, hqb_ref, hfb_ref, hvb_ref, lb_ref,
                  o_ref, of_ref, ob_ref, m_ref, acc_ref, sf_ref, sb_ref, *, nk):
    qi = pl.program_id(2)
    tq = q_ref.shape[2]
    tk = bias_ref.shape[3]
    n_trips = nk // KEY_GROUP
    g_chunks = tq // (CHUNK * n_trips)
    rows_g = g_chunks * CHUNK

    @pl.when(qi == 0)
    def _():
        sf_ref[...] = jnp.zeros(sf_ref.shape, F32)
        sb_ref[...] = jnp.zeros(sb_ref.shape, F32)

    lb_f = lb_ref[0:1, :]
    lb_b = lb_ref[1:2, :]

    def hgrn_step(g):
        row_f = pl.multiple_of(g * rows_g, rows_g)
        sf_ref[...] = _hgrn_group(hqf_ref, hff_ref, hvf_ref, of_ref, sf_ref[...], lb_f, row_f,
                                  g_chunks, False)
        row_b = pl.multiple_of((n_trips - 1 - g) * rows_g, rows_g)
        sb_ref[...] = _hgrn_group(hqb_ref, hfb_ref, hvb_ref, ob_ref, sb_ref[...], lb_b, row_b,
                                  g_chunks, True)

    qs = q_ref[:, 0].reshape(2 * tq, LANES)

    def scores(j):
        k = k_ref[0, pl.ds(pl.multiple_of(j * tk, tk), tk), :]
        return lax.dot_general(qs, k, (((1,), (1,)), ((), ())), preferred_element_type=F32)

    def values(j):
        return v_ref[0, pl.ds(pl.multiple_of(j * tk, tk), tk), :]

    def add_bias(s, bias):
        return (s.reshape(2, tq, tk) + bias[None]).reshape(2 * tq, tk)

    def for_tiles(lo, hi, fn):
        def body(j, carry):
            fn(j)
            return carry
        lax.fori_loop(lo, hi, body, 0)

    def biased_scores(j):
        d_idx = jnp.clip(j - qi * (tq // tk), BAND_OFFSETS[0], BAND_OFFSETS[-1]) - BAND_OFFSETS[0]
        return add_bias(scores(j), bias_ref[0, d_idx])

    @pl.when(fast_ref[0] == 1)
    def _():
        def group(g, first):
            for u in range(KEY_GROUP):
                j = g * KEY_GROUP + u
                p = jnp.exp2(biased_scores(j)).astype(BF16)
                pv = jnp.dot(p, values(j), preferred_element_type=F32)
                if first and u == 0:
                    acc_ref[...] = pv
                else:
                    acc_ref[...] += pv
            hgrn_step(g)

        group(0, True)
        for_tiles(1, n_trips, lambda g: group(g, False))

    @pl.when(fast_ref[0] != 1)
    def _():
        m_ref[...] = jnp.full(m_ref.shape, NEG_BIG, F32)
        acc_ref[...] = jnp.zeros(acc_ref.shape, F32)

        def update(j):
            s = biased_scores(j)
            m_prev = m_ref[...]
            m_new = jnp.maximum(m_prev, jnp.max(s, axis=-1, keepdims=True))
            alpha = jnp.exp2(m_prev - m_new)
            p = jnp.exp2(s - m_new).astype(BF16)
            acc_ref[...] = alpha * acc_ref[...] + jnp.dot(p, values(j), preferred_element_type=F32)
            m_ref[...] = m_new

        for_tiles(0, n_trips, hgrn_step)
        for_tiles(0, nk, update)

    acc = acc_ref[...]
    o1 = acc[:tq, :V_DIM] / acc[:tq, V_DIM:]
    o2 = acc[tq:, :V_DIM] / acc[tq:, V_DIM:]
    o = o1 - lam_ref[0] * o2
    ms = jnp.mean(o * o, axis=-1, keepdims=True)
    o = o * lax.rsqrt(ms + EPS) * sw_ref[...] * (1.0 - LAM_INIT)
    o_ref[0] = o.astype(BF16)


def _mixer_call(qa, ka, va, qh, ffw, fbw, ih, p):
    _, b, s, _ = qa.shape
    tq, tk = TQ, TK
    assert tq % tk == 0 and s % tq == 0 and A_HEADS == HGRN_HEADS
    nq, nk = s // tq, s // tk
    assert nk % KEY_GROUP == 0 and tq % (CHUNK * (nk // KEY_GROUP)) == 0
    n_d = len(BAND_OFFSETS)
    fwd = lambda hi, bi, qi, *_: (bi, qi, hi)
    bwd = lambda hi, bi, qi, *_: (bi, nq - 1 - qi, hi)
    head_all = lambda hi, bi, qi, *_: (bi, 0, hi)
    tile = (1, tq, LANES)
    grid_spec = pltpu.PrefetchScalarGridSpec(
        num_scalar_prefetch=2,
        grid=(A_HEADS, b, nq),
        in_specs=[
            pl.BlockSpec((2, 1, tq, LANES), lambda hi, bi, qi, *_: (0, bi, qi, hi)),
            pl.BlockSpec((1, s, LANES), head_all),
            pl.BlockSpec((1, s, V_AUG), head_all),
            pl.BlockSpec((1, n_d, tq, tk), lambda hi, bi, qi, *_: (hi, 0, 0, 0),
                         pipeline_mode=pl.Buffered(1)),
            pl.BlockSpec((1, V_DIM), lambda hi, bi, qi, *_: (0, 0)),
            pl.BlockSpec(tile, fwd), pl.BlockSpec(tile, fwd), pl.BlockSpec(tile, fwd),
            pl.BlockSpec(tile, bwd), pl.BlockSpec(tile, bwd), pl.BlockSpec(tile, bwd),
            pl.BlockSpec((2, LANES), lambda hi, bi, qi, *_: (0, hi)),
        ],
        out_specs=(pl.BlockSpec(tile, fwd), pl.BlockSpec(tile, fwd), pl.BlockSpec(tile, bwd)),
        scratch_shapes=[pltpu.VMEM((2 * tq, 1), F32), pltpu.VMEM((2 * tq, V_AUG), F32),
                        pltpu.VMEM((HGRN_DV, HGRN_DK), F32), pltpu.VMEM((HGRN_DV, HGRN_DK), F32)],
    )
    return pl.pallas_call(
        functools.partial(_mixer_kernel, nk=nk),
        out_shape=(jax.ShapeDtypeStruct((b, s, A_HEADS * V_DIM), BF16),
                   jax.ShapeDtypeStruct((b, s, SEG), F32),
                   jax.ShapeDtypeStruct((b, s, SEG), F32)),
        grid_spec=grid_spec,
        compiler_params=pltpu.CompilerParams(
            dimension_semantics=("arbitrary", "arbitrary", "arbitrary"),
            vmem_limit_bytes=VMEM_LIMIT),
        name="mixer",
    )(p["fast"], p["lam"], qa, ka, va, p["bias_tiles"], p["subln_w"],
      qh, ffw, ih, qh, fbw, ih, p["lb"])


def _hgrn_group(q_ref, f_ref, v_ref, o_ref, st, lb, row0, g_chunks, reverse):
    rows_g = g_chunks * CHUNK
    r = lax.broadcasted_iota(jnp.int32, (CHUNK, CHUNK), 0)
    c = lax.broadcasted_iota(jnp.int32, (CHUNK, CHUNK), 1)
    pos = lax.broadcasted_iota(jnp.int32, (rows_g, LANES), 0) % CHUNK
    nt = (((1,), (1,)), ((), ()))
    tn = (((0,), (0,)), ((), ()))
    rows = pl.ds(row0, rows_g)
    q = q_ref[0, rows, :].astype(F32)
    v = v_ref[0, rows, :]
    t = jnp.tanh(0.5 * f_ref[0, rows, :])
    half = 0.5 * (1.0 - lb)
    f = (lb + half) + half * t
    k = half * (1.0 - t)
    b = jnp.log(f)
    step = 1
    while step < CHUNK:
        if reverse:
            b = b + jnp.where(pos < CHUNK - step, pltpu.roll(b, rows_g - step, 0), 0.0)
        else:
            b = b + jnp.where(pos >= step, pltpu.roll(b, step, 0), 0.0)
        step *= 2
    last_row = 0 if reverse else CHUNK - 1
    mid_row = CHUNK // 2 if reverse else CHUNK // 2 - 1
    mask = (r <= c) if reverse else (r >= c)
    q3 = q.reshape(g_chunks, CHUNK, LANES)
    k3 = k.reshape(g_chunks, CHUNK, LANES)
    b3 = b.reshape(g_chunks, CHUNK, LANES)
    b_last = b3[:, last_row:last_row + 1, :]
    b_mid = b3[:, mid_row:mid_row + 1, :]
    decay = jnp.exp(b_last)
    q_t = (q3 * jnp.exp(b3 - b_mid)).astype(BF16)
    k_t = (k3 * jnp.exp(b_mid - b3)).astype(BF16)
    q_i = (q3 * jnp.exp(b3)).astype(BF16)
    k_d = (k3 * jnp.exp(b_last - b3)).astype(BF16)
    v3 = v.reshape(g_chunks, CHUNK, LANES)
    a = [jnp.where(mask, lax.dot_general(q_t[g], k_t[g], nt, preferred_element_type=F32),
                   0.0).astype(BF16) for g in range(g_chunks)]
    ds = [lax.dot_general(v3[g], k_d[g], tn, preferred_element_type=F32)
          for g in range(g_chunks)]
    o_intra = [jnp.dot(a[g], v3[g], preferred_element_type=F32) for g in range(g_chunks)]
    outs = [None] * g_chunks
    for g in (range(g_chunks - 1, -1, -1) if reverse else range(g_chunks)):
        outs[g] = o_intra[g] + lax.dot_general(q_i[g], st.astype(BF16), nt,
                                               preferred_element_type=F32)
        st = decay[g] * st + ds[g]
    o_ref[0, rows, :] = jnp.concatenate(outs, axis=0)
    return st


def _mlp_kernel(x_ref, oa_ref, of_ref, ob_ref, g_ref, hw_ref, wo_ref, mw_ref, w1_ref, w2_ref, y_ref):
    o = of_ref[...] + ob_ref[...]
    g = g_ref[...]
    parts = [oa_ref[...]]
    for hh in range(HGRN_HEADS):
        sl = slice(hh * HGRN_DV, (hh + 1) * HGRN_DV)
        oh = o[:, sl]
        ms = jnp.mean(oh * oh, axis=-1, keepdims=True)
        gh = g[:, sl]
        parts.append((oh * lax.rsqrt(ms + EPS) * hw_ref[...] * _silu(gh)).astype(BF16))
    cat = jnp.concatenate(parts, axis=1)
    x1 = x_ref[...] + jnp.dot(cat, wo_ref[...], preferred_element_type=F32)
    ms = jnp.mean(x1 * x1, axis=-1, keepdims=True)
    h2 = (x1 * lax.rsqrt(ms + EPS) * mw_ref[...]).astype(BF16)
    acc = x1
    for cc in range(D_FF // D_MODEL):
        sl = slice(cc * D_MODEL, (cc + 1) * D_MODEL)
        u = jnp.dot(h2, w1_ref[:, sl], preferred_element_type=F32)
        a = jnp.square(jnp.maximum(u, 0.0)).astype(BF16)
        acc = acc + jnp.dot(a, w2_ref[sl, :], preferred_element_type=F32)
    y_ref[...] = acc


def _mlp_call(x2d, oa, o_f, o_b, gh, p):
    t = x2d.shape[0]
    tm = TM_MLP
    assert t % tm == 0
    row = lambda i: (i, 0)
    const = lambda i: (0, 0)
    once = pl.Buffered(1)
    return pl.pallas_call(
        _mlp_kernel,
        out_shape=jax.ShapeDtypeStruct((t, D_MODEL), F32),
        grid=(t // tm,),
        in_specs=[
            pl.BlockSpec((tm, D_MODEL), row),
            pl.BlockSpec((tm, SEG), row),
            pl.BlockSpec((tm, SEG), row),
            pl.BlockSpec((tm, SEG), row),
            pl.BlockSpec((tm, SEG), row),
            pl.BlockSpec((1, HGRN_DV), const),
            pl.BlockSpec((D_MODEL, D_MODEL), const, pipeline_mode=once),
            pl.BlockSpec((1, D_MODEL), const),
            pl.BlockSpec((D_MODEL, D_FF), const, pipeline_mode=once),
            pl.BlockSpec((D_FF, D_MODEL), const, pipeline_mode=once),
        ],
        out_specs=pl.BlockSpec((tm, D_MODEL), row),
        compiler_params=pltpu.CompilerParams(
            dimension_semantics=("arbitrary",), vmem_limit_bytes=VMEM_LIMIT),
        name="out_mlp",
    )(x2d, oa, o_f, o_b, gh, p["hgrn_norm_w"], p["w_out"], p["mlp_norm_w"], p["w_mlp_in"], p["w_mlp_out"])


def _prepare(attn_norm_w, w_in, qk_norm_w, diff_lambda, diff_subln_w, rel_bias, hgrn_lb,
             hgrn_norm_w, w_out, mlp_norm_w, w_mlp_in, w_mlp_out):
    l = 0
    lam_p = diff_lambda[l].astype(F32)
    lam = jnp.exp(jnp.sum(lam_p[0] * lam_p[1])) - jnp.exp(jnp.sum(lam_p[2] * lam_p[3])) + LAM_INIT
    lb = jnp.cumsum(jax.nn.softmax(hgrn_lb.astype(F32), axis=1), axis=1)[:, l]

    rb = rel_bias.astype(F32)
    left = jnp.take(rb, _t5_bucket(jnp.array([BAND_OFFSETS[0] * TK])), axis=0)
    table = (rb - left) * LOG2E
    reach = max(-BAND_OFFSETS[0] * TK + TQ - 1, BAND_OFFSETS[-1] * TK + TK - 1)
    width = 2 * pl.next_power_of_2(reach + 1)
    u = jnp.arange(width)
    line_buckets = _t5_bucket(jnp.where(u < width // 2, u, u - width))[None].astype(jnp.int32)
    tiles = _bias_call(table, line_buckets)

    wq_eff = qk_norm_w[l, 0].astype(F32) * (QK_DIM ** -0.5 * LOG2E)
    wk_eff = qk_norm_w[l, 1].astype(F32)
    logit_bound = (BF16_SLACK * QK_DIM * jnp.max(jnp.abs(wq_eff * wk_eff))
                   + 2.0 * jnp.max(jnp.abs(table)))
    fast = (logit_bound <= FAST_EXP_RANGE).astype(jnp.int32).reshape(1)

    reps = SEG // QK_DIM
    grp = jnp.arange(GROUP_MAT) // QK_DIM
    group_mat = jnp.where(grp[:, None] == grp[None, :], 1.0 / QK_DIM, 0.0).astype(BF16)
    return dict(
        attn_norm_w=attn_norm_w[l][None].astype(F32),
        w_in=w_in[l].astype(BF16),
        group_mat=group_mat,
        wq=jnp.tile(wq_eff, reps)[None],
        wk=jnp.tile(wk_eff, reps)[None],
        fast=fast,
        lam=jnp.reshape(lam, (1,)).astype(F32),
        bias_tiles=tiles,
        subln_w=diff_subln_w[l][None].astype(F32),
        lb=lb,
        hgrn_norm_w=hgrn_norm_w[l][None].astype(F32),
        w_out=w_out[l].astype(BF16),
        mlp_norm_w=mlp_norm_w[l][None].astype(F32),
        w_mlp_in=w_mlp_in[l].astype(BF16),
        w_mlp_out=w_mlp_out[l].astype(BF16),
    )


def _layer(x, p):
    b, s, d = x.shape
    x2d = x.reshape(b * s, d)
    qa, ka, va, qh, ffw, fbw, ih, gh = _proj_call(x2d, p)
    r3 = lambda t: t.reshape(b, s, t.shape[-1])
    oa, o_f, o_b = _mixer_call(qa.reshape(2, b, s, SEG), r3(ka), r3(va), r3(qh), r3(ffw), r3(fbw),
                               r3(ih), p)
    y = _mlp_call(x2d, oa.reshape(b * s, -1), o_f.reshape(b * s, -1), o_b.reshape(b * s, -1), gh, p)
    return y.reshape(b, s, d)


def kernel(x_prompt, x_sample, attn_norm_w, w_in, qk_norm_w, diff_lambda, diff_subln_w, rel_bias,
           hgrn_lb, hgrn_norm_w, w_out, mlp_norm_w, w_mlp_in, w_mlp_out):
    p = _prepare(attn_norm_w, w_in, qk_norm_w, diff_lambda, diff_subln_w, rel_bias, hgrn_lb,
                 hgrn_norm_w, w_out, mlp_norm_w, w_mlp_in, w_mlp_out)
    return _layer(x_prompt, p), _layer(x_sample, p)
```

```python
import functools
import math

import jax
import jax.numpy as jnp
from jax import lax
from jax.experimental import pallas as pl
from jax.experimental.pallas import tpu as pltpu

F32 = jnp.float32
BF16 = jnp.bfloat16

D_MODEL = 1024
A_HEADS = 4
V_DIM = 128
QK_DIM = 64
HGRN_HEADS = 4
HGRN_DK = 128
HGRN_DV = 128
SEG = 512
N_SEG = 8
D_FF = 4 * D_MODEL
N_BUCKETS = 32
MAX_DISTANCE = 128
CHUNK = 128
EPS = 1e-6
LOG2E = math.log2(math.e)
LAM_INIT = 0.8 - 0.6 * math.exp(-0.3 * 0)

LANES = 128
V_AUG = 2 * LANES
GROUP_MAT = 2 * LANES
VMEM_LIMIT = 56 * 1024 * 1024

TM_PROJ = 1024
TM_MLP = 512
TQ = 1024
TK = 512
KEY_GROUP = 8
BAND_OFFSETS = tuple(range(-2, TQ // TK + 2))
assert TK >= MAX_DISTANCE and TQ % TK == 0
BIAS_ROWS = 64
NEG_BIG = -1e30
FAST_EXP_RANGE = 60.0
BF16_SLACK = 1.02


def _silu(x):
    return (0.5 * x) * (1.0 + jnp.tanh(0.5 * x))


def _t5_bucket(rel):
    nb = N_BUCKETS // 2
    max_exact = nb // 2
    ret = jnp.where(rel > 0, nb, 0)
    n = jnp.abs(rel)
    nf = jnp.maximum(n, 1).astype(jnp.float32)
    large = max_exact + (jnp.log(nf / max_exact) / math.log(MAX_DISTANCE / max_exact)
                         * (nb - max_exact)).astype(jnp.int32)
    large = jnp.minimum(large, nb - 1)
    return ret + jnp.where(n < max_exact, n, large)


def _proj_kernel(x_ref, nw_ref, w_ref, g_ref, wq_ref, wk_ref,
                 qa_ref, ka_ref, va_ref, qh_ref, ffw_ref, fbw_ref, ih_ref, gh_ref):
    x = x_ref[...]
    ms = jnp.mean(x * x, axis=-1, keepdims=True)
    h = (x * lax.rsqrt(ms + EPS) * nw_ref[...]).astype(BF16)

    def seg(i):
        return jnp.dot(h, w_ref[:, i * SEG:(i + 1) * SEG], preferred_element_type=F32)

    def group_norm(t, w):
        sq = (t * t).astype(BF16)
        half = g_ref.shape[0]
        ss = jnp.concatenate(
            [jnp.dot(sq[:, c0:c0 + half], g_ref[...], preferred_element_type=F32)
             for c0 in range(0, SEG, half)], axis=1)
        return (t * lax.rsqrt(ss + EPS) * w).astype(BF16)

    qa_ref[...] = group_norm(seg(0), wq_ref[...])
    ka_ref[...] = group_norm(seg(1), wk_ref[...])

    v = seg(2).astype(BF16)
    ones = jnp.ones((v.shape[0], V_AUG - V_DIM), BF16)
    for hh in range(A_HEADS):
        va_ref[:, hh * V_AUG:hh * V_AUG + V_DIM] = v[:, hh * V_DIM:(hh + 1) * V_DIM]
        va_ref[:, hh * V_AUG + V_DIM:(hh + 1) * V_AUG] = ones

    qh = seg(3)
    qh_ref[...] = (_silu(qh) * (HGRN_DK ** -0.5)).astype(BF16)
    ffw_ref[...] = seg(4)
    fbw_ref[...] = seg(5)
    ih_ref[...] = seg(6).astype(BF16)
    gh_ref[...] = seg(7)


def _proj_call(x2d, p):
    t = x2d.shape[0]
    tm = TM_PROJ
    assert t % tm == 0
    row = lambda i: (i, 0)
    const = lambda i: (0, 0)
    out_shape = (
        jax.ShapeDtypeStruct((t, SEG), BF16),
        jax.ShapeDtypeStruct((t, SEG), BF16),
        jax.ShapeDtypeStruct((t, A_HEADS * V_AUG), BF16),
        jax.ShapeDtypeStruct((t, SEG), BF16),
        jax.ShapeDtypeStruct((t, SEG), F32),
        jax.ShapeDtypeStruct((t, SEG), F32),
        jax.ShapeDtypeStruct((t, SEG), BF16),
        jax.ShapeDtypeStruct((t, SEG), F32),
    )
    out_specs = tuple(pl.BlockSpec((tm, s.shape[1]), row) for s in out_shape)
    return pl.pallas_call(
        _proj_kernel,
        out_shape=out_shape,
        grid=(t // tm,),
        in_specs=[
            pl.BlockSpec((tm, D_MODEL), row),
            pl.BlockSpec((1, D_MODEL), const),
            pl.BlockSpec((D_MODEL, N_SEG * SEG), const, pipeline_mode=pl.Buffered(1)),
            pl.BlockSpec((GROUP_MAT, GROUP_MAT), const),
            pl.BlockSpec((1, SEG), const),
            pl.BlockSpec((1, SEG), const),
        ],
        out_specs=out_specs,
        compiler_params=pltpu.CompilerParams(
            dimension_semantics=("arbitrary",), vmem_limit_bytes=VMEM_LIMIT),
        name="proj",
    )(x2d, p["attn_norm_w"], p["w_in"], p["group_mat"], p["wq"], p["wk"])


def _bias_kernel(tab_ref, bucket_ref, o_ref):
    hh = pl.program_id(0)
    _, n_d, tq, tk = o_ref.shape
    width = bucket_ref.shape[1]
    rows = BIAS_ROWS
    bucket = bucket_ref[...]
    line = jnp.zeros(bucket.shape, F32)
    for bb in range(N_BUCKETS):
        line = jnp.where(bucket == bb, tab_ref[bb, hh], line)
    spread = jnp.broadcast_to(line, (rows, width))

    def body(i, carry):
        r0 = pl.multiple_of(i * rows, rows)
        rot = pltpu.roll(spread, r0, 1, stride=1, stride_axis=0)
        for d_idx, d in enumerate(BAND_OFFSETS):
            c0 = (d * tk) % width
            o_ref[0, d_idx, pl.ds(r0, rows), :] = rot[:, c0:c0 + tk]
        return carry

    lax.fori_loop(0, tq // rows, body, 0)


def _bias_call(table, line_buckets):
    n_d = len(BAND_OFFSETS)
    width = line_buckets.shape[1]
    return pl.pallas_call(
        _bias_kernel,
        out_shape=jax.ShapeDtypeStruct((A_HEADS, n_d, TQ, TK), F32),
        grid=(A_HEADS,),
        in_specs=[
            pl.BlockSpec(memory_space=pltpu.SMEM),
            pl.BlockSpec((1, width), lambda hi: (0, 0)),
        ],
        out_specs=pl.BlockSpec((1, n_d, TQ, TK), lambda hi: (hi, 0, 0, 0)),
        compiler_params=pltpu.CompilerParams(
            dimension_semantics=("arbitrary",), vmem_limit_bytes=VMEM_LIMIT),
        name="bias_tiles",
    )(table, line_buckets)


def _mixer_kernel(fast_ref, lam_ref, q_ref, k_ref, v_ref, bias_ref, sw_ref,
                  hqf_ref, hff_ref, hvf_ref, hqb_ref, hfb_ref, hvb_ref, lb_ref,
                  o_ref, of_ref, ob_ref, m_ref, acc_ref, sf_ref, sb_ref, *, nk):
    qi = pl.program_id(2)
    n_q = pl.num_programs(2) - 1
    active = qi < n_q
    tq = q_ref.shape[1]
    tk = bias_ref.shape[3]
    n_trips = nk // KEY_GROUP
    g_chunks = tq // (CHUNK * n_trips)
    rows_g = g_chunks * CHUNK
    cur = lax.rem(qi, 2)
    acc_cur = acc_ref.at[cur]
    acc_prev = acc_ref.at[1 - cur]

    @pl.when(qi == 0)
    def _():
        sf_ref[...] = jnp.zeros(sf_ref.shape, F32)
        sb_ref[...] = jnp.zeros(sb_ref.shape, F32)
        acc_ref[1] = jnp.ones(acc_ref.shape[1:], F32)

    def write_out_previous():
        acc = acc_prev[...]
        o1 = acc[:tq, :V_DIM] / acc[:tq, V_DIM:]
        o2 = acc[tq:, :V_DIM] / acc[tq:, V_DIM:]
        o = o1 - lam_ref[0] * o2
        ms = jnp.mean(o * o, axis=-1, keepdims=True)
        o = o * lax.rsqrt(ms + EPS) * sw_ref[...] * (1.0 - LAM_INIT)
        o_ref[0] = o.astype(BF16)

    lb_f = lb_ref[0:1, :]
    lb_b = lb_ref[1:2, :]

    def hgrn_step(g):
        row_f = pl.multiple_of(g * rows_g, rows_g)
        sf_ref[...] = _hgrn_group(hqf_ref, hff_ref, hvf_ref, of_ref, sf_ref[...], lb_f, row_f,
                                  g_chunks, False)
        row_b = pl.multiple_of((n_trips - 1 - g) * rows_g, rows_g)
        sb_ref[...] = _hgrn_group(hqb_ref, hfb_ref, hvb_ref, ob_ref, sb_ref[...], lb_b, row_b,
                                  g_chunks, True)

    q = q_ref[0]
    lane = lax.broadcasted_iota(jnp.int32, q.shape, 1)
    zero = jnp.zeros_like(q)
    qs = jnp.concatenate([jnp.where(lane < QK_DIM, q, zero),
                          jnp.where(lane >= QK_DIM, q, zero)], axis=0)

    def scores(j):
        k = k_ref[0, pl.ds(pl.multiple_of(j * tk, tk), tk), :]
        return lax.dot_general(qs, k, (((1,), (1,)), ((), ())), preferred_element_type=F32)

    def values(j):
        return v_ref[0, pl.ds(pl.multiple_of(j * tk, tk), tk), :]

    def add_bias(s, bias):
        return (s.reshape(2, tq, tk) + bias[None]).reshape(2 * tq, tk)

    def for_tiles(lo, hi, fn):
        def body(j, carry):
            fn(j)
            return carry
        lax.fori_loop(lo, hi, body, 0)

    def biased_scores(j):
        d_idx = jnp.clip(j - qi * (tq // tk), BAND_OFFSETS[0], BAND_OFFSETS[-1]) - BAND_OFFSETS[0]
        return add_bias(scores(j), bias_ref[0, d_idx])

    @pl.when(jnp.logical_and(active, fast_ref[0] == 1))
    def _():
        def group(g, first):
            for u in range(KEY_GROUP):
                j = g * KEY_GROUP + u
                p = jnp.exp2(biased_scores(j)).astype(BF16)
                pv = jnp.dot(p, values(j), preferred_element_type=F32)
                if first and u == 0:
                    acc_cur[...] = pv
                else:
                    acc_cur[...] += pv
            if first:
                write_out_previous()
            hgrn_step(g)

        group(0, True)
        for_tiles(1, n_trips, lambda g: group(g, False))

    @pl.when(jnp.logical_and(active, fast_ref[0] != 1))
    def _():
        write_out_previous()
        m_ref[...] = jnp.full(m_ref.shape, NEG_BIG, F32)
        acc_cur[...] = jnp.zeros(acc_cur.shape, F32)

        def update(j):
            s = biased_scores(j)
            m_prev = m_ref[...]
            m_new = jnp.maximum(m_prev, jnp.max(s, axis=-1, keepdims=True))
            alpha = jnp.exp2(m_prev - m_new)
            p = jnp.exp2(s - m_new).astype(BF16)
            acc_cur[...] = alpha * acc_cur[...] + jnp.dot(p, values(j), preferred_element_type=F32)
            m_ref[...] = m_new

        for_tiles(0, n_trips, hgrn_step)
        for_tiles(0, nk, update)

    @pl.when(jnp.logical_not(active))
    def _():
        write_out_previous()


def _mixer_call(qa, ka, va, qh, ffw, fbw, ih, p):
    b, s, _ = qa.shape
    tq, tk = TQ, TK
    assert tq % tk == 0 and s % tq == 0 and A_HEADS == HGRN_HEADS
    nq, nk = s // tq, s // tk
    assert nk % KEY_GROUP == 0 and tq % (CHUNK * (nk // KEY_GROUP)) == 0
    n_d = len(BAND_OFFSETS)
    fwd = lambda hi, bi, qi, *_: (bi, jnp.minimum(qi, nq - 1), hi)
    bwd = lambda hi, bi, qi, *_: (bi, nq - 1 - jnp.minimum(qi, nq - 1), hi)
    late = lambda hi, bi, qi, *_: (bi, jnp.maximum(qi - 1, 0), hi)
    head_all = lambda hi, bi, qi, *_: (bi, 0, hi)
    tile = (1, tq, LANES)
    grid_spec = pltpu.PrefetchScalarGridSpec(
        num_scalar_prefetch=2,
        grid=(A_HEADS, b, nq + 1),
        in_specs=[
            pl.BlockSpec(tile, fwd),
            pl.BlockSpec((1, s, LANES), head_all),
            pl.BlockSpec((1, s, V_AUG), head_all),
            pl.BlockSpec((1, n_d, tq, tk), lambda hi, bi, qi, *_: (hi, 0, 0, 0),
                         pipeline_mode=pl.Buffered(1)),
            pl.BlockSpec((1, V_DIM), lambda hi, bi, qi, *_: (0, 0)),
            pl.BlockSpec(tile, fwd), pl.BlockSpec(tile, fwd), pl.BlockSpec(tile, fwd),
            pl.BlockSpec(tile, bwd), pl.BlockSpec(tile, bwd), pl.BlockSpec(tile, bwd),
            pl.BlockSpec((2, LANES), lambda hi, bi, qi, *_: (0, hi)),
        ],
        out_specs=(pl.BlockSpec(tile, late), pl.BlockSpec(tile, fwd), pl.BlockSpec(tile, bwd)),
        scratch_shapes=[pltpu.VMEM((2 * tq, 1), F32), pltpu.VMEM((2, 2 * tq, V_AUG), F32),
                        pltpu.VMEM((HGRN_DV, HGRN_DK), F32), pltpu.VMEM((HGRN_DV, HGRN_DK), F32)],
    )
    return pl.pallas_call(
        functools.partial(_mixer_kernel, nk=nk),
        out_shape=(jax.ShapeDtypeStruct((b, s, A_HEADS * V_DIM), BF16),
                   jax.ShapeDtypeStruct((b, s, SEG), F32),
                   jax.ShapeDtypeStruct((b, s, SEG), F32)),
        grid_spec=grid_spec,
        compiler_params=pltpu.CompilerParams(
            dimension_semantics=("arbitrary", "arbitrary", "arbitrary"),
            vmem_limit_bytes=VMEM_LIMIT),
        name="mixer",
    )(p["fast"], p["lam"], qa, ka, va, p["bias_tiles"], p["subln_w"],
      qh, ffw, ih, qh, fbw, ih, p["lb"])


def _hgrn_group(q_ref, f_ref, v_ref, o_ref, st, lb, row0, g_chunks, reverse):
    rows_g = g_chunks * CHUNK
    r = lax.broadcasted_iota(jnp.int32, (CHUNK, CHUNK), 0)
    c = lax.broadcasted_iota(jnp.int32, (CHUNK, CHUNK), 1)
    pos = lax.broadcasted_iota(jnp.int32, (rows_g, LANES), 0) % CHUNK
    nt = (((1,), (1,)), ((), ()))
    tn = (((0,), (0,)), ((), ()))
    rows = pl.ds(row0, rows_g)
    q = q_ref[0, rows, :].astype(F32)
    v = v_ref[0, rows, :]
    t = jnp.tanh(0.5 * f_ref[0, rows, :])
    half = 0.5 * (1.0 - lb)
    f = (lb + half) + half * t
    k = half * (1.0 - t)
    b = jnp.log(f)
    step = 1
    while step < CHUNK:
        if reverse:
            b = b + jnp.where(pos < CHUNK - step, pltpu.roll(b, rows_g - step, 0), 0.0)
        else:
            b = b + jnp.where(pos >= step, pltpu.roll(b, step, 0), 0.0)
        step *= 2
    last_row = 0 if reverse else CHUNK - 1
    mid_row = CHUNK // 2 if reverse else CHUNK // 2 - 1
    mask = (r <= c) if reverse else (r >= c)
    q3 = q.reshape(g_chunks, CHUNK, LANES)
    k3 = k.reshape(g_chunks, CHUNK, LANES)
    b3 = b.reshape(g_chunks, CHUNK, LANES)
    b_last = b3[:, last_row:last_row + 1, :]
    b_mid = b3[:, mid_row:mid_row + 1, :]
    decay = jnp.exp(b_last)
    q_t = (q3 * jnp.exp(b3 - b_mid)).astype(BF16)
    k_t = (k3 * jnp.exp(b_mid - b3)).astype(BF16)
    q_i = (q3 * jnp.exp(b3)).astype(BF16)
    k_d = (k3 * jnp.exp(b_last - b3)).astype(BF16)
    v3 = v.reshape(g_chunks, CHUNK, LANES)
    a = [jnp.where(mask, lax.dot_general(q_t[g], k_t[g], nt, preferred_element_type=F32),
                   0.0).astype(BF16) for g in range(g_chunks)]
    ds = [lax.dot_general(v3[g], k_d[g], tn, preferred_element_type=F32)
          for g in range(g_chunks)]
    o_intra = [jnp.dot(a[g], v3[g], preferred_element_type=F32) for g in range(g_chunks)]
    outs = [None] * g_chunks
    for g in (range(g_chunks - 1, -1, -1) if reverse else range(g_chunks)):
        outs[g] = o_intra[g] + lax.dot_general(q_i[g], st.astype(BF16), nt,
                                               preferred_element_type=F32)
        st = decay[g] * st + ds[g]
    o_ref[0, rows, :] = jnp.concatenate(outs, axis=0)
    return st


def _mlp_kernel(x_ref, oa_ref, of_ref, ob_ref, g_ref, hw_ref, wo_ref, mw_ref, w1_ref, w2_ref, y_ref):
    o = of_ref[...] + ob_ref[...]
    g = g_ref[...]
    parts = [oa_ref[...]]
    for hh in range(HGRN_HEADS):
        sl = slice(hh * HGRN_DV, (hh + 1) * HGRN_DV)
        oh = o[:, sl]
        ms = jnp.mean(oh * oh, axis=-1, keepdims=True)
        gh = g[:, sl]
        parts.append((oh * lax.rsqrt(ms + EPS) * hw_ref[...] * _silu(gh)).astype(BF16))
    cat = jnp.concatenate(parts, axis=1)
    x1 = x_ref[...] + jnp.dot(cat, wo_ref[...], preferred_element_type=F32)
    ms = jnp.mean(x1 * x1, axis=-1, keepdims=True)
    h2 = (x1 * lax.rsqrt(ms + EPS) * mw_ref[...]).astype(BF16)
    acc = x1
    for cc in range(D_FF // D_MODEL):
        sl = slice(cc * D_MODEL, (cc + 1) * D_MODEL)
        u = jnp.dot(h2, w1_ref[:, sl], preferred_element_type=F32)
        a = jnp.square(jnp.maximum(u, 0.0)).astype(BF16)
        acc = acc + jnp.dot(a, w2_ref[sl, :], preferred_element_type=F32)
    y_ref[...] = acc


def _mlp_call(x2d, oa, o_f, o_b, gh, p):
    t = x2d.shape[0]
    tm = TM_MLP
    assert t % tm == 0
    row = lambda i: (i, 0)
    const = lambda i: (0, 0)
    once = pl.Buffered(1)
    return pl.pallas_call(
        _mlp_kernel,
        out_shape=jax.ShapeDtypeStruct((t, D_MODEL), F32),
        grid=(t // tm,),
        in_specs=[
            pl.BlockSpec((tm, D_MODEL), row),
            pl.BlockSpec((tm, SEG), row),
            pl.BlockSpec((tm, SEG), row),
            pl.BlockSpec((tm, SEG), row),
            pl.BlockSpec((tm, SEG), row),
            pl.BlockSpec((1, HGRN_DV), const),
            pl.BlockSpec((D_MODEL, D_MODEL), const, pipeline_mode=once),
            pl.BlockSpec((1, D_MODEL), const),
            pl.BlockSpec((D_MODEL, D_FF), const, pipeline_mode=once),
            pl.BlockSpec((D_FF, D_MODEL), const, pipeline_mode=once),
        ],
        out_specs=pl.BlockSpec((tm, D_MODEL), row),
        compiler_params=pltpu.CompilerParams(
            dimension_semantics=("arbitrary",), vmem_limit_bytes=VMEM_LIMIT),
        name="out_mlp",
    )(x2d, oa, o_f, o_b, gh, p["hgrn_norm_w"], p["w_out"], p["mlp_norm_w"], p["w_mlp_in"], p["w_mlp_out"])


def _prepare(attn_norm_w, w_in, qk_norm_w, diff_lambda, diff_subln_w, rel_bias, hgrn_lb,
             hgrn_norm_w, w_out, mlp_norm_w, w_mlp_in, w_mlp_out):
    l = 0
    lam_p = diff_lambda[l].astype(F32)
    lam = jnp.exp(jnp.sum(lam_p[0] * lam_p[1])) - jnp.exp(jnp.sum(lam_p[2] * lam_p[3])) + LAM_INIT
    lb = jnp.cumsum(jax.nn.softmax(hgrn_lb.astype(F32), axis=1), axis=1)[:, l]

    rb = rel_bias.astype(F32)
    left = jnp.take(rb, _t5_bucket(jnp.array([BAND_OFFSETS[0] * TK])), axis=0)
    table = (rb - left) * LOG2E
    reach = max(-BAND_OFFSETS[0] * TK + TQ - 1, BAND_OFFSETS[-1] * TK + TK - 1)
    width = 2 * pl.next_power_of_2(reach + 1)
    u = jnp.arange(width)
    line_buckets = _t5_bucket(jnp.where(u < width // 2, u, u - width))[None].astype(jnp.int32)
    tiles = _bias_call(table, line_buckets)

    wq_eff = qk_norm_w[l, 0].astype(F32) * (QK_DIM ** -0.5 * LOG2E)
    wk_eff = qk_norm_w[l, 1].astype(F32)
    logit_bound = (BF16_SLACK * QK_DIM * jnp.max(jnp.abs(wq_eff * wk_eff))
                   + 2.0 * jnp.max(jnp.abs(table)))
    fast = (logit_bound <= FAST_EXP_RANGE).astype(jnp.int32).reshape(1)

    reps = SEG // QK_DIM
    grp = jnp.arange(GROUP_MAT) // QK_DIM
    group_mat = jnp.where(grp[:, None] == grp[None, :], 1.0 / QK_DIM, 0.0).astype(BF16)
    return dict(
        attn_norm_w=attn_norm_w[l][None].astype(F32),
        w_in=w_in[l].astype(BF16),
        group_mat=group_mat,
        wq=jnp.tile(wq_eff, reps)[None],
        wk=jnp.tile(wk_eff, reps)[None],
        fast=fast,
        lam=jnp.reshape(lam, (1,)).astype(F32),
        bias_tiles=tiles,
        subln_w=diff_subln_w[l][None].astype(F32),
        lb=lb,
        hgrn_norm_w=hgrn_norm_w[l][None].astype(F32),
        w_out=w_out[l].astype(BF16),
        mlp_norm_w=mlp_norm_w[l][None].astype(F32),
        w_mlp_in=w_mlp_in[l].astype(BF16),
        w_mlp_out=w_mlp_out[l].astype(BF16),
    )


def _layer(x, p):
    b, s, d = x.shape
    x2d = x.reshape(b * s, d)
    qa, ka, va, qh, ffw, fbw, ih, gh = _proj_call(x2d, p)
    r3 = lambda t: t.reshape(b, s, t.shape[-1])
    oa, o_f, o_b = _mixer_call(r3(qa), r3(ka), r3(va), r3(qh), r3(ffw), r3(fbw), r3(ih), p)
    y = _mlp_call(x2d, oa.reshape(b * s, -1), o_f.reshape(b * s, -1), o_b.reshape(b * s, -1), gh, p)
    return y.reshape(b, s, d)


def kernel(x_prompt, x_sample, attn_norm_w, w_in, qk_norm_w, diff_lambda, diff_subln_w, rel_bias,
           hgrn_lb, hgrn_norm_w, w_out, mlp_norm_w, w_mlp_in, w_mlp_out):
    p = _prepare(attn_norm_w, w_in, qk_norm_w, diff_lambda, diff_subln_w, rel_bias, hgrn_lb,
                 hgrn_norm_w, w_out, mlp_norm_w, w_mlp_in, w_mlp_out)
    return _layer(x_prompt, p), _layer(x_sample, p)
```

```python
import functools
import math

import jax
import jax.numpy as jnp
from jax import lax
from jax.experimental import pallas as pl
from jax.experimental.pallas import tpu as pltpu

F32 = jnp.float32
BF16 = jnp.bfloat16

D_MODEL = 1024
A_HEADS = 4
V_DIM = 128
QK_DIM = 64
HGRN_HEADS = 4
HGRN_DK = 128
HGRN_DV = 128
SEG = 512
N_SEG = 8
D_FF = 4 * D_MODEL
N_BUCKETS = 32
MAX_DISTANCE = 128
CHUNK = 128
EPS = 1e-6
LOG2E = math.log2(math.e)
LAM_INIT = 0.8 - 0.6 * math.exp(-0.3 * 0)

LANES = 128
V_AUG = 2 * LANES
GROUP_MAT = 2 * LANES
VMEM_LIMIT = 56 * 1024 * 1024

TM_PROJ = 1024
TM_MLP = 512
TQ = 1024
TK = 512
KEY_GROUP = 8
BAND_OFFSETS = tuple(range(-2, TQ // TK + 2))
assert TK >= MAX_DISTANCE and TQ % TK == 0
BIAS_ROWS = 64
NEG_BIG = -1e30
FAST_EXP_RANGE = 60.0
BF16_SLACK = 1.02


def _silu(x):
    return (0.5 * x) * (1.0 + jnp.tanh(0.5 * x))


def _t5_bucket(rel):
    nb = N_BUCKETS // 2
    max_exact = nb // 2
    ret = jnp.where(rel > 0, nb, 0)
    n = jnp.abs(rel)
    nf = jnp.maximum(n, 1).astype(jnp.float32)
    large = max_exact + (jnp.log(nf / max_exact) / math.log(MAX_DISTANCE / max_exact)
                         * (nb - max_exact)).astype(jnp.int32)
    large = jnp.minimum(large, nb - 1)
    return ret + jnp.where(n < max_exact, n, large)


def _proj_kernel(x_ref, nw_ref, w_ref, g_ref, wq_ref, wk_ref,
                 qa_ref, ka_ref, va_ref, qh_ref, ffw_ref, fbw_ref, ih_ref, gh_ref):
    x = x_ref[...]
    ms = jnp.mean(x * x, axis=-1, keepdims=True)
    h = (x * lax.rsqrt(ms + EPS) * nw_ref[...]).astype(BF16)

    def seg(i):
        return jnp.dot(h, w_ref[:, i * SEG:(i + 1) * SEG], preferred_element_type=F32)

    def group_norm(t, w):
        sq = (t * t).astype(BF16)
        half = g_ref.shape[0]
        ss = jnp.concatenate(
            [jnp.dot(sq[:, c0:c0 + half], g_ref[...], preferred_element_type=F32)
             for c0 in range(0, SEG, half)], axis=1)
        return (t * lax.rsqrt(ss + EPS) * w).astype(BF16)

    qa_ref[...] = group_norm(seg(0), wq_ref[...])
    ka_ref[...] = group_norm(seg(1), wk_ref[...])

    v = seg(2).astype(BF16)
    ones = jnp.ones((v.shape[0], V_AUG - V_DIM), BF16)
    for hh in range(A_HEADS):
        va_ref[:, hh * V_AUG:hh * V_AUG + V_DIM] = v[:, hh * V_DIM:(hh + 1) * V_DIM]
        va_ref[:, hh * V_AUG + V_DIM:(hh + 1) * V_AUG] = ones

    qh = seg(3)
    qh_ref[...] = (_silu(qh) * (HGRN_DK ** -0.5)).astype(BF16)
    ffw_ref[...] = seg(4)
    fbw_ref[...] = seg(5)
    ih_ref[...] = seg(6).astype(BF16)
    gh_ref[...] = seg(7)


def _proj_call(x2d, p):
    t = x2d.shape[0]
    tm = TM_PROJ
    assert t % tm == 0
    row = lambda i: (i, 0)
    const = lambda i: (0, 0)
    out_shape = (
        jax.ShapeDtypeStruct((t, SEG), BF16),
        jax.ShapeDtypeStruct((t, SEG), BF16),
        jax.ShapeDtypeStruct((t, A_HEADS * V_AUG), BF16),
        jax.ShapeDtypeStruct((t, SEG), BF16),
        jax.ShapeDtypeStruct((t, SEG), F32),
        jax.ShapeDtypeStruct((t, SEG), F32),
        jax.ShapeDtypeStruct((t, SEG), BF16),
        jax.ShapeDtypeStruct((t, SEG), F32),
    )
    out_specs = tuple(pl.BlockSpec((tm, s.shape[1]), row) for s in out_shape)
    return pl.pallas_call(
        _proj_kernel,
        out_shape=out_shape,
        grid=(t // tm,),
        in_specs=[
            pl.BlockSpec((tm, D_MODEL), row),
            pl.BlockSpec((1, D_MODEL), const),
            pl.BlockSpec((D_MODEL, N_SEG * SEG), const, pipeline_mode=pl.Buffered(1)),
            pl.BlockSpec((GROUP_MAT, GROUP_MAT), const),
            pl.BlockSpec((1, SEG), const),
            pl.BlockSpec((1, SEG), const),
        ],
        out_specs=out_specs,
        compiler_params=pltpu.CompilerParams(
            dimension_semantics=("arbitrary",), vmem_limit_bytes=VMEM_LIMIT,
            allow_input_fusion=[False, False, True, False, False, False]),
        name="proj",
    )(x2d, p["attn_norm_w"], p["w_in"], p["group_mat"], p["wq"], p["wk"])


def _bias_kernel(tab_ref, bucket_ref, o_ref):
    hh = pl.program_id(0)
    _, n_d, tq, tk = o_ref.shape
    width = bucket_ref.shape[1]
    rows = BIAS_ROWS
    bucket = bucket_ref[...]
    line = jnp.zeros(bucket.shape, F32)
    for bb in range(N_BUCKETS):
        line = jnp.where(bucket == bb, tab_ref[bb, hh], line)
    spread = jnp.broadcast_to(line, (rows, width))

    def body(i, carry):
        r0 = pl.multiple_of(i * rows, rows)
        rot = pltpu.roll(spread, r0, 1, stride=1, stride_axis=0)
        for d_idx, d in enumerate(BAND_OFFSETS):
            c0 = (d * tk) % width
            o_ref[0, d_idx, pl.ds(r0, rows), :] = rot[:, c0:c0 + tk]
        return carry

    lax.fori_loop(0, tq // rows, body, 0)


def _bias_call(table, line_buckets):
    n_d = len(BAND_OFFSETS)
    width = line_buckets.shape[1]
    return pl.pallas_call(
        _bias_kernel,
        out_shape=jax.ShapeDtypeStruct((A_HEADS, n_d, TQ, TK), F32),
        grid=(A_HEADS,),
        in_specs=[
            pl.BlockSpec(memory_space=pltpu.SMEM),
            pl.BlockSpec((1, width), lambda hi: (0, 0)),
        ],
        out_specs=pl.BlockSpec((1, n_d, TQ, TK), lambda hi: (hi, 0, 0, 0)),
        compiler_params=pltpu.CompilerParams(
            dimension_semantics=("arbitrary",), vmem_limit_bytes=VMEM_LIMIT),
        name="bias_tiles",
    )(table, line_buckets)


def _mixer_kernel(fast_ref, lam_ref, q_ref, k_ref, v_ref, bias_ref, sw_ref,
                  hqf_ref, hff_ref, hvf_ref, hqb_ref, hfb_ref, hvb_ref, lb_ref,
                  o_ref, of_ref, ob_ref, m_ref, acc_ref, sf_ref, sb_ref, *, nk):
    qi = pl.program_id(2)
    tq = q_ref.shape[1]
    tk = bias_ref.shape[3]
    n_trips = nk // KEY_GROUP
    g_chunks = tq // (CHUNK * n_trips)
    rows_g = g_chunks * CHUNK

    @pl.when(qi == 0)
    def _():
        sf_ref[...] = jnp.zeros(sf_ref.shape, F32)
        sb_ref[...] = jnp.zeros(sb_ref.shape, F32)

    lb_f = lb_ref[0:1, :]
    lb_b = lb_ref[1:2, :]

    def hgrn_step(g):
        row_f = pl.multiple_of(g * rows_g, rows_g)
        sf_ref[...] = _hgrn_group(hqf_ref, hff_ref, hvf_ref, of_ref, sf_ref[...], lb_f, row_f,
                                  g_chunks, False)
        row_b = pl.multiple_of((n_trips - 1 - g) * rows_g, rows_g)
        sb_ref[...] = _hgrn_group(hqb_ref, hfb_ref, hvb_ref, ob_ref, sb_ref[...], lb_b, row_b,
                                  g_chunks, True)

    q = q_ref[0]
    lane = lax.broadcasted_iota(jnp.int32, q.shape, 1)
    zero = jnp.zeros_like(q)
    qs = jnp.concatenate([jnp.where(lane < QK_DIM, q, zero),
                          jnp.where(lane >= QK_DIM, q, zero)], axis=0)

    def scores(j):
        k = k_ref[0, pl.ds(pl.multiple_of(j * tk, tk), tk), :]
        return lax.dot_general(qs, k, (((1,), (1,)), ((), ())), preferred_element_type=F32)

    def values(j):
        return v_ref[0, pl.ds(pl.multiple_of(j * tk, tk), tk), :]

    def add_bias(s, bias):
        return (s.reshape(2, tq, tk) + bias[None]).reshape(2 * tq, tk)

    def for_tiles(lo, hi, fn):
        def body(j, carry):
            fn(j)
            return carry
        lax.fori_loop(lo, hi, body, 0)

    def biased_scores(j):
        d_idx = jnp.clip(j - qi * (tq // tk), BAND_OFFSETS[0], BAND_OFFSETS[-1]) - BAND_OFFSETS[0]
        return add_bias(scores(j), bias_ref[0, d_idx])

    @pl.when(fast_ref[0] == 1)
    def _():
        def group(g, first):
            for u in range(KEY_GROUP):
                j = g * KEY_GROUP + u
                p = jnp.exp2(biased_scores(j)).astype(BF16)
                pv = jnp.dot(p, values(j), preferred_element_type=F32)
                if first and u == 0:
                    acc_ref[...] = pv
                else:
                    acc_ref[...] += pv
            hgrn_step(g)

        group(0, True)
        for_tiles(1, n_trips, lambda g: group(g, False))

    @pl.when(fast_ref[0] != 1)
    def _():
        m_ref[...] = jnp.full(m_ref.shape, NEG_BIG, F32)
        acc_ref[...] = jnp.zeros(acc_ref.shape, F32)

        def update(j):
            s = biased_scores(j)
            m_prev = m_ref[...]
            m_new = jnp.maximum(m_prev, jnp.max(s, axis=-1, keepdims=True))
            alpha = jnp.exp2(m_prev - m_new)
            p = jnp.exp2(s - m_new).astype(BF16)
            acc_ref[...] = alpha * acc_ref[...] + jnp.dot(p, values(j), preferred_element_type=F32)
            m_ref[...] = m_new

        for_tiles(0, n_trips, hgrn_step)
        for_tiles(0, nk, update)

    acc = acc_ref[...]
    o1 = acc[:tq, :V_DIM] / acc[:tq, V_DIM:]
    o2 = acc[tq:, :V_DIM] / acc[tq:, V_DIM:]
    o = o1 - lam_ref[0] * o2
    ms = jnp.mean(o * o, axis=-1, keepdims=True)
    o = o * lax.rsqrt(ms + EPS) * sw_ref[...] * (1.0 - LAM_INIT)
    o_ref[0] = o.astype(BF16)


def _mixer_call(qa, ka, va, qh, ffw, fbw, ih, p):
    b, s, _ = qa.shape
    tq, tk = TQ, TK
    assert tq % tk == 0 and s % tq == 0 and A_HEADS == HGRN_HEADS
    nq, nk = s // tq, s // tk
    assert nk % KEY_GROUP == 0 and tq % (CHUNK * (nk // KEY_GROUP)) == 0
    n_d = len(BAND_OFFSETS)
    fwd = lambda hi, bi, qi, *_: (bi, qi, hi)
    bwd = lambda hi, bi, qi, *_: (bi, nq - 1 - qi, hi)
    head_all = lambda hi, bi, qi, *_: (bi, 0, hi)
    tile = (1, tq, LANES)
    grid_spec = pltpu.PrefetchScalarGridSpec(
        num_scalar_prefetch=2,
        grid=(A_HEADS, b, nq),
        in_specs=[
            pl.BlockSpec(tile, fwd),
            pl.BlockSpec((1, s, LANES), head_all),
            pl.BlockSpec((1, s, V_AUG), head_all),
            pl.BlockSpec((1, n_d, tq, tk), lambda hi, bi, qi, *_: (hi, 0, 0, 0),
                         pipeline_mode=pl.Buffered(1)),
            pl.BlockSpec((1, V_DIM), lambda hi, bi, qi, *_: (0, 0)),
            pl.BlockSpec(tile, fwd), pl.BlockSpec(tile, fwd), pl.BlockSpec(tile, fwd),
            pl.BlockSpec(tile, bwd), pl.BlockSpec(tile, bwd), pl.BlockSpec(tile, bwd),
            pl.BlockSpec((2, LANES), lambda hi, bi, qi, *_: (0, hi)),
        ],
        out_specs=(pl.BlockSpec(tile, fwd), pl.BlockSpec(tile, fwd), pl.BlockSpec(tile, bwd)),
        scratch_shapes=[pltpu.VMEM((2 * tq, 1), F32), pltpu.VMEM((2 * tq, V_AUG), F32),
                        pltpu.VMEM((HGRN_DV, HGRN_DK), F32), pltpu.VMEM((HGRN_DV, HGRN_DK), F32)],
    )
    return pl.pallas_call(
        functools.partial(_mixer_kernel, nk=nk),
        out_shape=(jax.ShapeDtypeStruct((b, s, A_HEADS * V_DIM), BF16),
                   jax.ShapeDtypeStruct((b, s, SEG), F32),
                   jax.ShapeDtypeStruct((b, s, SEG), F32)),
        grid_spec=grid_spec,
        compiler_params=pltpu.CompilerParams(
            dimension_semantics=("arbitrary", "arbitrary", "arbitrary"),
            vmem_limit_bytes=VMEM_LIMIT),
        name="mixer",
    )(p["fast"], p["lam"], qa, ka, va, p["bias_tiles"], p["subln_w"],
      qh, ffw, ih, qh, fbw, ih, p["lb"])


def _hgrn_group(q_ref, f_ref, v_ref, o_ref, st, lb, row0, g_chunks, reverse):
    rows_g = g_chunks * CHUNK
    r = lax.broadcasted_iota(jnp.int32, (CHUNK, CHUNK), 0)
    c = lax.broadcasted_iota(jnp.int32, (CHUNK, CHUNK), 1)
    pos = lax.broadcasted_iota(jnp.int32, (rows_g, LANES), 0) % CHUNK
    nt = (((1,), (1,)), ((), ()))
    tn = (((0,), (0,)), ((), ()))
    rows = pl.ds(row0, rows_g)
    q = q_ref[0, rows, :].astype(F32)
    v = v_ref[0, rows, :]
    t = jnp.tanh(0.5 * f_ref[0, rows, :])
    half = 0.5 * (1.0 - lb)
    f = (lb + half) + half * t
    k = half * (1.0 - t)
    b = jnp.log(f)
    step = 1
    while step < CHUNK:
        if reverse:
            b = b + jnp.where(pos < CHUNK - step, pltpu.roll(b, rows_g - step, 0), 0.0)
        else:
            b = b + jnp.where(pos >= step, pltpu.roll(b, step, 0), 0.0)
        step *= 2
    last_row = 0 if reverse else CHUNK - 1
    mid_row = CHUNK // 2 if reverse else CHUNK // 2 - 1
    mask = (r <= c) if reverse else (r >= c)
    q3 = q.reshape(g_chunks, CHUNK, LANES)
    k3 = k.reshape(g_chunks, CHUNK, LANES)
    b3 = b.reshape(g_chunks, CHUNK, LANES)
    b_last = b3[:, last_row:last_row + 1, :]
    b_mid = b3[:, mid_row:mid_row + 1, :]
    decay = jnp.exp(b_last)
    q_t = (q3 * jnp.exp(b3 - b_mid)).astype(BF16)
    k_t = (k3 * jnp.exp(b_mid - b3)).astype(BF16)
    q_i = (q3 * jnp.exp(b3)).astype(BF16)
    k_d = (k3 * jnp.exp(b_last - b3)).astype(BF16)
    v3 = v.reshape(g_chunks, CHUNK, LANES)
    a = [jnp.where(mask, lax.dot_general(q_t[g], k_t[g], nt, preferred_element_type=F32),
                   0.0).astype(BF16) for g in range(g_chunks)]
    ds = [lax.dot_general(v3[g], k_d[g], tn, preferred_element_type=F32)
          for g in range(g_chunks)]
    o_intra = [jnp.dot(a[g], v3[g], preferred_element_type=F32) for g in range(g_chunks)]
    outs = [None] * g_chunks
    for g in (range(g_chunks - 1, -1, -1) if reverse else range(g_chunks)):
        outs[g] = o_intra[g] + lax.dot_general(q_i[g], st.astype(BF16), nt,
                                               preferred_element_type=F32)
        st = decay[g] * st + ds[g]
    o_ref[0, rows, :] = jnp.concatenate(outs, axis=0)
    return st


def _mlp_kernel(x_ref, oa_ref, of_ref, ob_ref, g_ref, hw_ref, wo_ref, mw_ref, w1_ref, w2_ref, y_ref):
    o = of_ref[...] + ob_ref[...]
    g = g_ref[...]
    parts = [oa_ref[...]]
    for hh in range(HGRN_HEADS):
        sl = slice(hh * HGRN_DV, (hh + 1) * HGRN_DV)
        oh = o[:, sl]
        ms = jnp.mean(oh * oh, axis=-1, keepdims=True)
        gh = g[:, sl]
        parts.append((oh * lax.rsqrt(ms + EPS) * hw_ref[...] * _silu(gh)).astype(BF16))
    cat = jnp.concatenate(parts, axis=1)
    x1 = x_ref[...] + jnp.dot(cat, wo_ref[...], preferred_element_type=F32)
    ms = jnp.mean(x1 * x1, axis=-1, keepdims=True)
    h2 = (x1 * lax.rsqrt(ms + EPS) * mw_ref[...]).astype(BF16)
    acc = x1
    for cc in range(D_FF // D_MODEL):
        sl = slice(cc * D_MODEL, (cc + 1) * D_MODEL)
        u = jnp.dot(h2, w1_ref[:, sl], preferred_element_type=F32)
        a = jnp.square(jnp.maximum(u, 0.0)).astype(BF16)
        acc = acc + jnp.dot(a, w2_ref[sl, :], preferred_element_type=F32)
    y_ref[...] = acc


def _mlp_call(x2d, oa, o_f, o_b, gh, p):
    t = x2d.shape[0]
    tm = TM_MLP
    assert t % tm == 0
    row = lambda i: (i, 0)
    const = lambda i: (0, 0)
    once = pl.Buffered(1)
    return pl.pallas_call(
        _mlp_kernel,
        out_shape=jax.ShapeDtypeStruct((t, D_MODEL), F32),
        grid=(t // tm,),
        in_specs=[
            pl.BlockSpec((tm, D_MODEL), row),
            pl.BlockSpec((tm, SEG), row),
            pl.BlockSpec((tm, SEG), row),
            pl.BlockSpec((tm, SEG), row),
            pl.BlockSpec((tm, SEG), row),
            pl.BlockSpec((1, HGRN_DV), const),
            pl.BlockSpec((D_MODEL, D_MODEL), const, pipeline_mode=once),
            pl.BlockSpec((1, D_MODEL), const),
            pl.BlockSpec((D_MODEL, D_FF), const, pipeline_mode=once),
            pl.BlockSpec((D_FF, D_MODEL), const, pipeline_mode=once),
        ],
        out_specs=pl.BlockSpec((tm, D_MODEL), row),
        compiler_params=pltpu.CompilerParams(
            dimension_semantics=("arbitrary",), vmem_limit_bytes=VMEM_LIMIT,
            allow_input_fusion=[False] * 6 + [True, False, True, True]),
        name="out_mlp",
    )(x2d, oa, o_f, o_b, gh, p["hgrn_norm_w"], p["w_out"], p["mlp_norm_w"], p["w_mlp_in"], p["w_mlp_out"])


def _prepare(attn_norm_w, w_in, qk_norm_w, diff_lambda, diff_subln_w, rel_bias, hgrn_lb,
             hgrn_norm_w, w_out, mlp_norm_w, w_mlp_in, w_mlp_out):
    l = 0
    lam_p = diff_lambda[l].astype(F32)
    lam = jnp.exp(jnp.sum(lam_p[0] * lam_p[1])) - jnp.exp(jnp.sum(lam_p[2] * lam_p[3])) + LAM_INIT
    lb = jnp.cumsum(jax.nn.softmax(hgrn_lb.astype(F32), axis=1), axis=1)[:, l]

    rb = rel_bias.astype(F32)
    left = jnp.take(rb, _t5_bucket(jnp.array([BAND_OFFSETS[0] * TK])), axis=0)
    table = (rb - left) * LOG2E
    reach = max(-BAND_OFFSETS[0] * TK + TQ - 1, BAND_OFFSETS[-1] * TK + TK - 1)
    width = 2 * pl.next_power_of_2(reach + 1)
    u = jnp.arange(width)
    line_buckets = _t5_bucket(jnp.where(u < width // 2, u, u - width))[None].astype(jnp.int32)
    tiles = _bias_call(table, line_buckets)

    wq_eff = qk_norm_w[l, 0].astype(F32) * (QK_DIM ** -0.5 * LOG2E)
    wk_eff = qk_norm_w[l, 1].astype(F32)
    logit_bound = (BF16_SLACK * QK_DIM * jnp.max(jnp.abs(wq_eff * wk_eff))
                   + 2.0 * jnp.max(jnp.abs(table)))
    fast = (logit_bound <= FAST_EXP_RANGE).astype(jnp.int32).reshape(1)

    reps = SEG // QK_DIM
    grp = jnp.arange(GROUP_MAT) // QK_DIM
    group_mat = jnp.where(grp[:, None] == grp[None, :], 1.0 / QK_DIM, 0.0).astype(BF16)
    return dict(
        attn_norm_w=attn_norm_w[l][None].astype(F32),
        w_in=w_in[l].astype(BF16),
        group_mat=group_mat,
        wq=jnp.tile(wq_eff, reps)[None],
        wk=jnp.tile(wk_eff, reps)[None],
        fast=fast,
        lam=jnp.reshape(lam, (1,)).astype(F32),
        bias_tiles=tiles,
        subln_w=diff_subln_w[l][None].astype(F32),
        lb=lb,
        hgrn_norm_w=hgrn_norm_w[l][None].astype(F32),
        w_out=w_out[l].astype(BF16),
        mlp_norm_w=mlp_norm_w[l][None].astype(F32),
        w_mlp_in=w_mlp_in[l].astype(BF16),
        w_mlp_out=w_mlp_out[l].astype(BF16),
    )


def _layer(x, p):
    b, s, d = x.shape
    x2d = x.reshape(b * s, d)
    qa, ka, va, qh, ffw, fbw, ih, gh = _proj_call(x2d, p)
    r3 = lambda t: t.reshape(b, s, t.shape[-1])
    oa, o_f, o_b = _mixer_call(r3(qa), r3(ka), r3(va), r3(qh), r3(ffw), r3(fbw), r3(ih), p)
    y = _mlp_call(x2d, oa.reshape(b * s, -1), o_f.reshape(b * s, -1), o_b.reshape(b * s, -1), gh, p)
    return y.reshape(b, s, d)


def kernel(x_prompt, x_sample, attn_norm_w, w_in, qk_norm_w, diff_lambda, diff_subln_w, rel_bias,
           hgrn_lb, hgrn_norm_w, w_out, mlp_norm_w, w_mlp_in, w_mlp_out):
    p = _prepare(attn_norm_w, w_in, qk_norm_w, diff_lambda, diff_subln_w, rel_bias, hgrn_lb,
                 hgrn_norm_w, w_out, mlp_norm_w, w_mlp_in, w_mlp_out)
    return _layer(x_prompt, p), _layer(x_sample, p)
```
